```python
import jax, jax.numpy as jnp
from jax import lax
import numpy as np

D_MODEL = 2048
BATCH = 2
SEQ = 8192
DEPTH = 2

N_MIXERS = 2
CONV_WIDTH = 3
CHUNK = 128
SG_GROUPS = 8
SG_WIDTH = D_MODEL
D_FF = 5632
RMS_EPS = 1e-5
N_A = (DEPTH + 1) // 2
N_B = DEPTH // 2

kernel_name = "hybrid_shortconv_spatialgate_convffn"


def rmsnorm(x, g):
    xf = x.astype(jnp.float32)
    inv = lax.rsqrt(jnp.mean(xf * xf, axis=-1, keepdims=True) + RMS_EPS)
    return (xf * inv).astype(x.dtype) * g


def causal_dwconv3(x, w):
    s = x.shape[1]
    xp = jnp.pad(x, ((0, 0), (CONV_WIDTH - 1, 0), (0, 0)))
    return xp[:, :s] * w[0] + xp[:, 1:s + 1] * w[1] + xp[:, 2:s + 2] * w[2]


def short_conv_mixer(h, w_in, w_conv, w_out):
    bcx = jnp.einsum('bsd,de->bse', h, w_in)
    gb, gc, xs = jnp.split(bcx, 3, axis=-1)
    y = gb * causal_dwconv3(gc * xs, w_conv)
    return jnp.einsum('bsd,de->bse', y, w_out)


def spatial_gating_mixer(h, w_in, v_norm, w_s, b_s, w_out):
    bsz, s, _ = h.shape
    z = jax.nn.gelu(jnp.einsum('bsd,de->bse', h, w_in))
    u, v = jnp.split(z, 2, axis=-1)
    v = rmsnorm(v, v_norm)
    n_chunks = s // CHUNK
    vr = v.reshape(bsz, n_chunks, CHUNK, SG_GROUPS, SG_WIDTH // SG_GROUPS)
    mask = jnp.tril(jnp.ones((CHUNK, CHUNK), dtype=w_s.dtype))
    ws = w_s * mask
    mixed = jnp.einsum('hts,bnshc->bnthc', ws, vr) + b_s.T[None, None, :, :, None]
    gate = mixed.reshape(bsz, s, SG_WIDTH)
    return jnp.einsum('bsd,de->bse', u * gate, w_out)


def conv_ffn(h, w_up, conv_w, conv_b, w_down):
    up = jnp.einsum('bsd,df->bsf', h, w_up)
    up = causal_dwconv3(up, conv_w) + conv_b
    g, a = jnp.split(up, 2, axis=-1)
    return jnp.einsum('bsf,fd->bsd', jax.nn.silu(g) * a, w_down)


def setup_inputs(seed: int = 0) -> dict:
    key = jax.random.key(seed)
    ks = jax.random.split(key, 20)
    f32 = jnp.float32
    D = D_MODEL
    def nrm(k, shape, scale):
        return jax.random.normal(k, shape, f32) * scale
    def gain(k, shape):
        return 1.0 + 0.02 * jax.random.normal(k, shape, f32)
    return {
        "x": nrm(ks[0], (BATCH, SEQ, D), 1.0),
        "a_norm": gain(ks[1], (N_A, D)),
        "a_in": nrm(ks[2], (N_A, D, 3 * D), D ** -0.5),
        "a_conv": nrm(ks[3], (N_A, CONV_WIDTH, D), CONV_WIDTH ** -0.5),
        "a_out": nrm(ks[4], (N_A, D, D), D ** -0.5),
        "b_norm": gain(ks[5], (N_B, D)),
        "b_in": nrm(ks[6], (N_B, D, 2 * SG_WIDTH), D ** -0.5),
        "b_vnorm": gain(ks[7], (N_B, SG_WIDTH)),
        "b_ws": nrm(ks[8], (N_B, SG_GROUPS, CHUNK, CHUNK), CHUNK ** -0.5),
        "b_bs": gain(ks[9], (N_B, SG_GROUPS, CHUNK)),
        "b_out": nrm(ks[10], (N_B, SG_WIDTH, D), SG_WIDTH ** -0.5),
        "f_norm": gain(ks[11], (DEPTH, D)),
        "f_up": nrm(ks[12], (DEPTH, D, 2 * D_FF), D ** -0.5),
        "f_conv_w": nrm(ks[13], (DEPTH, CONV_WIDTH, 2 * D_FF), CONV_WIDTH ** -0.5),
        "f_conv_b": nrm(ks[14], (DEPTH, 2 * D_FF), 0.01),
        "f_down": nrm(ks[15], (DEPTH, D_FF, D), D_FF ** -0.5),
        "final_norm": gain(ks[16], (D,)),
    }


def reference(x, a_norm, a_in, a_conv, a_out, b_norm, b_in, b_vnorm, b_ws, b_bs, b_out,
              f_norm, f_up, f_conv_w, f_conv_b, f_down, final_norm):
    for i in range(DEPTH):
        j = i // N_MIXERS
        if i % N_MIXERS == 0:
            h = rmsnorm(x, a_norm[j])
            x = x + short_conv_mixer(h, a_in[j], a_conv[j], a_out[j])
        else:
            h = rmsnorm(x, b_norm[j])
            x = x + spatial_gating_mixer(h, b_in[j], b_vnorm[j], b_ws[j], b_bs[j], b_out[j])
        h = rmsnorm(x, f_norm[i])
        x = x + conv_ffn(h, f_up[i], f_conv_w[i], f_conv_b[i], f_down[i])
    return rmsnorm(x, final_norm)
```

```python
import functools

import jax
import jax.numpy as jnp
from jax import lax
from jax.experimental import pallas as pl
from jax.experimental.pallas import tpu as pltpu

RMS_EPS = 1e-5
CHUNK = 128
SUBLANES = 8
ROW_TILE = 512
FFN_TILE = 512
CONV_TILE = 512
SG_TILE = 1024
VMEM_LIMIT_BYTES = 56 * 1024 * 1024


def _rmsnorm(x, g):
    inv = lax.rsqrt(jnp.mean(x * x, axis=-1, keepdims=True) + RMS_EPS)
    return (x * inv) * g


def _bf16_dot(a, b):
    return jnp.dot(a, b, preferred_element_type=jnp.float32)


def _causal_conv3(cur, prev_tail, w):
    ext = jnp.concatenate([prev_tail, cur], axis=0)
    m1 = pltpu.roll(ext, 1, axis=0)[SUBLANES:]
    m2 = pltpu.roll(ext, 2, axis=0)[SUBLANES:]
    return m2 * w[0:1] + m1 * w[1:2] + cur * w[2:3]


def _carried_tail(carry_ref, j, seq_start, cur):
    @pl.when(seq_start)
    def _():
        carry_ref[j] = jnp.zeros(carry_ref.shape[1:], carry_ref.dtype)

    prev = carry_ref[j]
    carry_ref[j] = cur[cur.shape[0] - SUBLANES:]
    return prev


def _short_conv_kernel(x_ref, g_ref, win_ref, cw_ref, wout_ref, o_ref, h_ref, carry_ref,
                       *, tiles_per_seq, tn):
    i, j = pl.program_id(0), pl.program_id(1)

    @pl.when(j == 0)
    def _():
        x = x_ref[...]
        h_ref[...] = _rmsnorm(x, g_ref[...]).astype(h_ref.dtype)
        o_ref[...] = x

    bcx = _bf16_dot(h_ref[...], win_ref[...])
    p = bcx[:, tn:2 * tn] * bcx[:, 2 * tn:]
    prev = _carried_tail(carry_ref, j, i % tiles_per_seq == 0, p)
    y = bcx[:, :tn] * _causal_conv3(p, prev, cw_ref[...])
    o_ref[...] += _bf16_dot(y.astype(jnp.bfloat16), wout_ref[...])


def _short_conv_block(x, g, w_in, w_conv, w_out, *, seq):
    rows, d = x.shape
    tm, tn = ROW_TILE, CONV_TILE
    nj = d // tn
    win = w_in.astype(jnp.bfloat16).reshape(d, 3, nj, tn).transpose(2, 0, 1, 3).reshape(nj, d, 3 * tn)
    cw = w_conv.reshape(3, nj, tn).transpose(1, 0, 2)
    body = functools.partial(_short_conv_kernel, tiles_per_seq=seq // tm, tn=tn)
    return pl.pallas_call(
        body,
        grid=(rows // tm, nj),
        in_specs=[
            pl.BlockSpec((tm, d), lambda i, j: (i, 0)),
            pl.BlockSpec((1, d), lambda i, j: (0, 0)),
            pl.BlockSpec((None, d, 3 * tn), lambda i, j: (j, 0, 0)),
            pl.BlockSpec((None, 3, tn), lambda i, j: (j, 0, 0)),
            pl.BlockSpec((tn, d), lambda i, j: (j, 0)),
        ],
        out_specs=pl.BlockSpec((tm, d), lambda i, j: (i, 0)),
        out_shape=jax.ShapeDtypeStruct((rows, d), jnp.float32),
        scratch_shapes=[
            pltpu.VMEM((tm, d), jnp.bfloat16),
            pltpu.VMEM((nj, SUBLANES, tn), jnp.float32),
        ],
        compiler_params=pltpu.CompilerParams(
            dimension_semantics=("arbitrary", "arbitrary"),
            vmem_limit_bytes=VMEM_LIMIT_BYTES),
        name="short_conv_mixer",
    )(x, g.reshape(1, d), win, cw, w_out.astype(jnp.bfloat16))


def _conv_ffn_kernel(x_ref, g_ref, wup_ref, cw_ref, cb_ref, wdown_ref, fg_ref, o_ref,
                     h_ref, carry_ref, *, tiles_per_seq, tf, final_norm):
    i, j = pl.program_id(0), pl.program_id(1)

    @pl.when(j == 0)
    def _():
        x = x_ref[...]
        h_ref[...] = _rmsnorm(x, g_ref[...]).astype(h_ref.dtype)
        o_ref[...] = x

    up = _bf16_dot(h_ref[...], wup_ref[...])
    prev = _carried_tail(carry_ref, j, i % tiles_per_seq == 0, up)
    c = _causal_conv3(up, prev, cw_ref[...]) + cb_ref[...]
    act = jax.nn.silu(c[:, :tf]) * c[:, tf:]
    o_ref[...] += _bf16_dot(act.astype(jnp.bfloat16), wdown_ref[...])

    if final_norm:
        @pl.when(j == pl.num_programs(1) - 1)
        def _():
            o_ref[...] = _rmsnorm(o_ref[...], fg_ref[...])


def _conv_ffn_block(x, g, w_up, conv_w, conv_b, w_down, final_g, *, seq, final_norm):
    rows, d = x.shape
    d_ff = w_down.shape[0]
    tm, tf = ROW_TILE, FFN_TILE
    nj = d_ff // tf

    def pair_blocks(a):
        lead = a.shape[:-1]
        a = a.reshape(lead + (2, nj, tf))
        a = jnp.moveaxis(a, -2, 0)
        return a.reshape((nj,) + lead + (2 * tf,))

    wup = pair_blocks(w_up.astype(jnp.bfloat16))
    cw = pair_blocks(conv_w)
    cb = pair_blocks(conv_b.reshape(1, 2 * d_ff))
    body = functools.partial(_conv_ffn_kernel, tiles_per_seq=seq // tm, tf=tf, final_norm=final_norm)
    return pl.pallas_call(
        body,
        grid=(rows // tm, nj),
        in_specs=[
            pl.BlockSpec((tm, d), lambda i, j: (i, 0)),
            pl.BlockSpec((1, d), lambda i, j: (0, 0)),
            pl.BlockSpec((None, d, 2 * tf), lambda i, j: (j, 0, 0)),
            pl.BlockSpec((None, 3, 2 * tf), lambda i, j: (j, 0, 0)),
            pl.BlockSpec((None, 1, 2 * tf), lambda i, j: (j, 0, 0)),
            pl.BlockSpec((tf, d), lambda i, j: (j, 0)),
            pl.BlockSpec((1, d), lambda i, j: (0, 0)),
        ],
        out_specs=pl.BlockSpec((tm, d), lambda i, j: (i, 0)),
        out_shape=jax.ShapeDtypeStruct((rows, d), jnp.float32),
        scratch_shapes=[
            pltpu.VMEM((tm, d), jnp.bfloat16),
            pltpu.VMEM((nj, SUBLANES, 2 * tf), jnp.float32),
        ],
        compiler_params=pltpu.CompilerParams(
            dimension_semantics=("arbitrary", "arbitrary"),
            vmem_limit_bytes=VMEM_LIMIT_BYTES),
        name="conv_ffn",
    )(x, g.reshape(1, d), wup, cw, cb, w_down.astype(jnp.bfloat16), final_g.reshape(1, d))


def _spatial_gate_kernel(x_ref, g_ref, win_ref, vg_ref, ws_ref, bst_ref, wout_ref, o_ref,
                         h_ref, uv_ref, ug_ref, *, n_in, n_out, groups):
    j = pl.program_id(1)
    tm, tc = uv_ref.shape[1], uv_ref.shape[2]
    half_blocks = n_in // 2
    gw = (half_blocks * tc) // groups
    groups_per_block = tc // gw

    @pl.when(j == 0)
    def _():
        h_ref[...] = _rmsnorm(x_ref[...], g_ref[...]).astype(h_ref.dtype)

    @pl.when(j < n_in)
    def _():
        uv_ref[j] = jax.nn.gelu(_bf16_dot(h_ref[...], win_ref[...]))

    @pl.when(j == n_in - 1)
    def _():
        ssq = jnp.zeros((tm, 1), jnp.float32)
        for b in range(half_blocks):
            v = uv_ref[half_blocks + b]
            ssq += jnp.sum(v * v, axis=-1, keepdims=True)
        inv = lax.rsqrt(ssq / (half_blocks * tc) + RMS_EPS)
        row = lax.broadcasted_iota(jnp.int32, (CHUNK, CHUNK), 0)
        col = lax.broadcasted_iota(jnp.int32, (CHUNK, CHUNK), 1)
        tril = (row >= col).astype(jnp.float32)
        for b in range(half_blocks):
            vn = ((uv_ref[half_blocks + b] * inv) * vg_ref[:, b * tc:(b + 1) * tc]).astype(jnp.bfloat16)
            for gl in range(groups_per_block):
                hg = b * groups_per_block + gl
                ws = (ws_ref[hg] * tril).astype(jnp.bfloat16)
                bias = bst_ref[:, hg:hg + 1]
                for c in range(tm // CHUNK):
                    rs = slice(c * CHUNK, (c + 1) * CHUNK)
                    gate = _bf16_dot(ws, vn[rs, gl * gw:(gl + 1) * gw]) + bias
                    u = uv_ref[b, rs, gl * gw:(gl + 1) * gw]
                    ug_ref[rs, hg * gw:(hg + 1) * gw] = (u * gate).astype(ug_ref.dtype)

    for k in range(n_out):
        @pl.when(j == n_in + k)
        def _(k=k):
            cs = slice(k * tc, (k + 1) * tc)
            o_ref[:, cs] = x_ref[:, cs] + _bf16_dot(ug_ref[...], wout_ref[...])


def _spatial_gate_block(x, g, w_in, v_norm, w_s, b_s, w_out):
    rows, d = x.shape
    width = w_out.shape[0]
    groups = w_s.shape[0]
    tm, tc = ROW_TILE, SG_TILE
    n_in, n_out = 2 * width // tc, d // tc

    def col_blocks(w, n):
        return w.astype(jnp.bfloat16).reshape(w.shape[0], n, tc).transpose(1, 0, 2)

    body = functools.partial(_spatial_gate_kernel, n_in=n_in, n_out=n_out, groups=groups)
    return pl.pallas_call(
        body,
        grid=(rows // tm, n_in + n_out),
        in_specs=[
            pl.BlockSpec((tm, d), lambda i, j: (i, 0)),
            pl.BlockSpec((1, d), lambda i, j: (0, 0)),
            pl.BlockSpec((None, d, tc), lambda i, j: (jnp.minimum(j, n_in - 1), 0, 0)),
            pl.BlockSpec((1, width), lambda i, j: (0, 0)),
            pl.BlockSpec((groups, CHUNK, CHUNK), lambda i, j: (0, 0, 0)),
            pl.BlockSpec((CHUNK, groups), lambda i, j: (0, 0)),
            pl.BlockSpec((None, width, tc), lambda i, j: (jnp.maximum(j - n_in, 0), 0, 0)),
        ],
        out_specs=pl.BlockSpec((tm, d), lambda i, j: (i, 0)),
        out_shape=jax.ShapeDtypeStruct((rows, d), jnp.float32),
        scratch_shapes=[
            pltpu.VMEM((tm, d), jnp.bfloat16),
            pltpu.VMEM((n_in, tm, tc), jnp.float32),
            pltpu.VMEM((tm, width), jnp.bfloat16),
        ],
        compiler_params=pltpu.CompilerParams(
            dimension_semantics=("arbitrary", "arbitrary"),
            vmem_limit_bytes=VMEM_LIMIT_BYTES),
        name="spatial_gate_mixer",
    )(x, g.reshape(1, d), col_blocks(w_in, n_in), v_norm.reshape(1, width), w_s, b_s.T,
      col_blocks(w_out, n_out))


def kernel(x, a_norm, a_in, a_conv, a_out, b_norm, b_in, b_vnorm, b_ws, b_bs, b_out,
           f_norm, f_up, f_conv_w, f_conv_b, f_down, final_norm):
    batch, seq, d = x.shape
    depth = f_norm.shape[0]
    assert seq % ROW_TILE == 0 and ROW_TILE % CHUNK == 0
    h = x.reshape(batch * seq, d)
    for layer in range(depth):
        m = layer // 2
        if layer % 2 == 0:
            h = _short_conv_block(h, a_norm[m], a_in[m], a_conv[m], a_out[m], seq=seq)
        else:
            h = _spatial_gate_block(h, b_norm[m], b_in[m], b_vnorm[m], b_ws[m], b_bs[m], b_out[m])
        h = _conv_ffn_block(h, f_norm[layer], f_up[layer], f_conv_w[layer], f_conv_b[layer],
                            f_down[layer], final_norm, seq=seq, final_norm=(layer == depth - 1))
    return h.reshape(batch, seq, d)
```

```python
import functools

import jax
import jax.numpy as jnp
from jax import lax
from jax.experimental import pallas as pl
from jax.experimental.pallas import tpu as pltpu

RMS_EPS = 1e-5
CHUNK = 128
SUBLANES = 8
LANES = 128
ROW_TILE = 512
FFN_TILE = 512
CONV_TILE = 512
SG_TILE = 1024
VMEM_LIMIT_BYTES = 56 * 1024 * 1024
STAGE_ORDER = "gpc"


def _rmsnorm(x, g):
    inv = lax.rsqrt(jnp.mean(x * x, axis=-1, keepdims=True) + RMS_EPS)
    return (x * inv) * g


def _bf16_dot(a, b):
    return jnp.dot(a, b, preferred_element_type=jnp.float32)


def _causal_conv3(cur, prev_tail, w):
    ext = jnp.concatenate([prev_tail, cur], axis=0)
    m1 = pltpu.roll(ext, 1, axis=0)[SUBLANES:]
    m2 = pltpu.roll(ext, 2, axis=0)[SUBLANES:]
    return m2 * w[0:1] + m1 * w[1:2] + cur * w[2:3]


def _carried_tail(carry_ref, j, seq_start, cur):
    @pl.when(seq_start)
    def _():
        carry_ref[j] = jnp.zeros(carry_ref.shape[1:], carry_ref.dtype)

    prev = carry_ref[j]
    carry_ref[j] = cur[cur.shape[0] - SUBLANES:]
    return prev


def _short_conv_kernel(x_ref, g_ref, win_ref, cw_ref, wout_ref, o_ref, h_ref, carry_ref,
                       *, tiles_per_seq, tn):
    i, j = pl.program_id(0), pl.program_id(1)

    @pl.when(j == 0)
    def _():
        x = x_ref[...]
        h_ref[...] = _rmsnorm(x, g_ref[...]).astype(h_ref.dtype)
        o_ref[...] = x

    bcx = _bf16_dot(h_ref[...], win_ref[...])
    p = bcx[:, tn:2 * tn] * bcx[:, 2 * tn:]
    prev = _carried_tail(carry_ref, j, i % tiles_per_seq == 0, p)
    y = bcx[:, :tn] * _causal_conv3(p, prev, cw_ref[...])
    o_ref[...] += _bf16_dot(y.astype(jnp.bfloat16), wout_ref[...])


def _short_conv_block(x, g, w_in, w_conv, w_out, *, seq):
    rows, d = x.shape
    tm, tn = ROW_TILE, CONV_TILE
    nj = d // tn
    win = w_in.astype(jnp.bfloat16).reshape(d, 3, nj, tn).transpose(2, 0, 1, 3).reshape(nj, d, 3 * tn)
    cw = w_conv.reshape(3, nj, tn).transpose(1, 0, 2)
    body = functools.partial(_short_conv_kernel, tiles_per_seq=seq // tm, tn=tn)
    return pl.pallas_call(
        body,
        grid=(rows // tm, nj),
        in_specs=[
            pl.BlockSpec((tm, d), lambda i, j: (i, 0)),
            pl.BlockSpec((1, d), lambda i, j: (0, 0)),
            pl.BlockSpec((None, d, 3 * tn), lambda i, j: (j, 0, 0)),
            pl.BlockSpec((None, 3, tn), lambda i, j: (j, 0, 0)),
            pl.BlockSpec((tn, d), lambda i, j: (j, 0)),
        ],
        out_specs=pl.BlockSpec((tm, d), lambda i, j: (i, 0)),
        out_shape=jax.ShapeDtypeStruct((rows, d), jnp.float32),
        scratch_shapes=[
            pltpu.VMEM((tm, d), jnp.bfloat16),
            pltpu.VMEM((nj, SUBLANES, tn), jnp.float32),
        ],
        compiler_params=pltpu.CompilerParams(
            dimension_semantics=("arbitrary", "arbitrary"),
            vmem_limit_bytes=VMEM_LIMIT_BYTES),
        name="short_conv_mixer",
    )(x, g.reshape(1, d), win, cw, w_out.astype(jnp.bfloat16))


def _conv_ffn_kernel(x_ref, g_ref, wg_ref, wa_ref, cwg_ref, cwa_ref, cbg_ref, cba_ref, wdown_ref,
                     fg_ref, o_ref, h_ref, raw_even, raw_odd, act_even, act_odd, carry_ref,
                     *, tiles_per_seq, tf, nj, final_norm):
    i, j = pl.program_id(0), pl.program_id(1)
    raws, acts = (raw_even, raw_odd), (act_even, act_odd)
    tm = h_ref.shape[0]

    def project(raw):
        h = h_ref[...]
        raw[SUBLANES:, :tf] = _bf16_dot(h, wg_ref[...])
        raw[SUBLANES:, tf:] = _bf16_dot(h, wa_ref[...])

    def gate(raw, act):
        tail = carry_ref.at[j - 1]
        raw[:SUBLANES, :] = tail[...]
        tail[...] = raw[tm:, :]
        rb = 16
        for c in range(tf // LANES):
            cs = slice(c * LANES, (c + 1) * LANES)
            wts = [(cw_ref[:, cs], cb_ref[:, cs]) for cw_ref, cb_ref in ((cwg_ref, cbg_ref), (cwa_ref, cba_ref))]
            for r in range(tm // rb):
                parts = []
                for p, (w, b) in enumerate(wts):
                    rs = slice(p * tf + c * LANES, p * tf + (c + 1) * LANES)
                    r0 = SUBLANES + r * rb
                    parts.append(raw[r0 - 2:r0 - 2 + rb, rs] * w[0:1]
                                 + raw[r0 - 1:r0 - 1 + rb, rs] * w[1:2]
                                 + raw[r0:r0 + rb, rs] * w[2:3] + b)
                act[r * rb:(r + 1) * rb, cs] = (jax.nn.silu(parts[0]) * parts[1]).astype(act.dtype)

    def contract(act):
        o_ref[...] += _bf16_dot(act[...], wdown_ref[...])

    def step(parity, do_project, do_gate, do_contract):
        for stage in STAGE_ORDER:
            if stage == "g" and do_gate:
                gate(raws[1 - parity], acts[1 - parity])
            if stage == "c" and do_contract:
                contract(acts[parity])
            if stage == "p" and do_project:
                project(raws[parity])

    @pl.when(j == 0)
    def _():
        x = x_ref[...]
        h_ref[...] = _rmsnorm(x, g_ref[...]).astype(h_ref.dtype)
        o_ref[...] = x

        @pl.when(i % tiles_per_seq == 0)
        def _():
            carry_ref[...] = jnp.zeros(carry_ref.shape, carry_ref.dtype)

        step(0, True, False, False)

    @pl.when(j == 1)
    def _():
        step(1, True, True, False)

    for parity in range(2):
        @pl.when((j >= 2) & (j < nj) & (j % 2 == parity))
        def _(parity=parity):
            step(parity, True, True, True)

    @pl.when(j == nj)
    def _():
        step(nj % 2, False, True, True)

    @pl.when(j == nj + 1)
    def _():
        step((nj + 1) % 2, False, False, True)
        if final_norm:
            o_ref[...] = _rmsnorm(o_ref[...], fg_ref[...])


def _conv_ffn_block(x, g, w_up, conv_w, conv_b, w_down, final_g, *, seq, final_norm):
    rows, d = x.shape
    d_ff = w_down.shape[0]
    tm, tf = ROW_TILE, FFN_TILE
    nj = d_ff // tf
    wup = w_up.astype(jnp.bfloat16)
    cb = conv_b.reshape(1, 2 * d_ff)
    body = functools.partial(_conv_ffn_kernel, tiles_per_seq=seq // tm, tf=tf, nj=nj,
                             final_norm=final_norm)

    def blk(j, lag):
        return jnp.clip(j - lag, 0, nj - 1)

    return pl.pallas_call(
        body,
        grid=(rows // tm, nj + 2),
        in_specs=[
            pl.BlockSpec((tm, d), lambda i, j: (i, 0)),
            pl.BlockSpec((1, d), lambda i, j: (0, 0)),
            pl.BlockSpec((d, tf), lambda i, j: (0, blk(j, 0))),
            pl.BlockSpec((d, tf), lambda i, j: (0, nj + blk(j, 0))),
            pl.BlockSpec((3, tf), lambda i, j: (0, blk(j, 1))),
            pl.BlockSpec((3, tf), lambda i, j: (0, nj + blk(j, 1))),
            pl.BlockSpec((1, tf), lambda i, j: (0, blk(j, 1))),
            pl.BlockSpec((1, tf), lambda i, j: (0, nj + blk(j, 1))),
            pl.BlockSpec((tf, d), lambda i, j: (blk(j, 2), 0)),
            pl.BlockSpec((1, d), lambda i, j: (0, 0)),
        ],
        out_specs=pl.BlockSpec((tm, d), lambda i, j: (i, 0)),
        out_shape=jax.ShapeDtypeStruct((rows, d), jnp.float32),
        scratch_shapes=[
            pltpu.VMEM((tm, d), jnp.bfloat16),
            pltpu.VMEM((tm + SUBLANES, 2 * tf), jnp.float32),
            pltpu.VMEM((tm + SUBLANES, 2 * tf), jnp.float32),
            pltpu.VMEM((tm, tf), jnp.bfloat16),
            pltpu.VMEM((tm, tf), jnp.bfloat16),
            pltpu.VMEM((nj, SUBLANES, 2 * tf), jnp.float32),
        ],
        compiler_params=pltpu.CompilerParams(
            dimension_semantics=("arbitrary", "arbitrary"),
            vmem_limit_bytes=VMEM_LIMIT_BYTES),
        name="conv_ffn",
    )(x, g.reshape(1, d), wup, wup, conv_w, conv_w, cb, cb, w_down.astype(jnp.bfloat16),
      final_g.reshape(1, d))


def _spatial_gate_kernel(x_ref, g_ref, win_ref, vg_ref, ws_ref, bst_ref, wout_ref, o_ref,
                         h_ref, uv_ref, ug_ref, *, n_in, n_out, groups):
    j = pl.program_id(1)
    tm, tc = uv_ref.shape[1], uv_ref.shape[2]
    half_blocks = n_in // 2
    gw = (half_blocks * tc) // groups
    groups_per_block = tc // gw

    @pl.when(j == 0)
    def _():
        h_ref[...] = _rmsnorm(x_ref[...], g_ref[...]).astype(h_ref.dtype)

    @pl.when(j < n_in)
    def _():
        uv_ref[j] = jax.nn.gelu(_bf16_dot(h_ref[...], win_ref[...]))

    @pl.when(j == n_in - 1)
    def _():
        ssq = jnp.zeros((tm, 1), jnp.float32)
        for b in range(half_blocks):
            v = uv_ref[half_blocks + b]
            ssq += jnp.sum(v * v, axis=-1, keepdims=True)
        inv = lax.rsqrt(ssq / (half_blocks * tc) + RMS_EPS)
        row = lax.broadcasted_iota(jnp.int32, (CHUNK, CHUNK), 0)
        col = lax.broadcasted_iota(jnp.int32, (CHUNK, CHUNK), 1)
        tril = (row >= col).astype(jnp.float32)
        for b in range(half_blocks):
            vn = ((uv_ref[half_blocks + b] * inv) * vg_ref[:, b * tc:(b + 1) * tc]).astype(jnp.bfloat16)
            for gl in range(groups_per_block):
                hg = b * groups_per_block + gl
                ws = (ws_ref[hg] * tril).astype(jnp.bfloat16)
                bias = bst_ref[:, hg:hg + 1]
                for c in range(tm // CHUNK):
                    rs = slice(c * CHUNK, (c + 1) * CHUNK)
                    gate = _bf16_dot(ws, vn[rs, gl * gw:(gl + 1) * gw]) + bias
                    u = uv_ref[b, rs, gl * gw:(gl + 1) * gw]
                    ug_ref[rs, hg * gw:(hg + 1) * gw] = (u * gate).astype(ug_ref.dtype)

    for k in range(n_out):
        @pl.when(j == n_in + k)
        def _(k=k):
            cs = slice(k * tc, (k + 1) * tc)
            o_ref[:, cs] = x_ref[:, cs] + _bf16_dot(ug_ref[...], wout_ref[...])


def _spatial_gate_block(x, g, w_in, v_norm, w_s, b_s, w_out):
    rows, d = x.shape
    width = w_out.shape[0]
    groups = w_s.shape[0]
    tm, tc = ROW_TILE, SG_TILE
    n_in, n_out = 2 * width // tc, d // tc

    def col_blocks(w, n):
        return w.astype(jnp.bfloat16).reshape(w.shape[0], n, tc).transpose(1, 0, 2)

    body = functools.partial(_spatial_gate_kernel, n_in=n_in, n_out=n_out, groups=groups)
    return pl.pallas_call(
        body,
        grid=(rows // tm, n_in + n_out),
        in_specs=[
            pl.BlockSpec((tm, d), lambda i, j: (i, 0)),
            pl.BlockSpec((1, d), lambda i, j: (0, 0)),
            pl.BlockSpec((None, d, tc), lambda i, j: (jnp.minimum(j, n_in - 1), 0, 0)),
            pl.BlockSpec((1, width), lambda i, j: (0, 0)),
            pl.BlockSpec((groups, CHUNK, CHUNK), lambda i, j: (0, 0, 0)),
            pl.BlockSpec((CHUNK, groups), lambda i, j: (0, 0)),
            pl.BlockSpec((None, width, tc), lambda i, j: (jnp.maximum(j - n_in, 0), 0, 0)),
        ],
        out_specs=pl.BlockSpec((tm, d), lambda i, j: (i, 0)),
        out_shape=jax.ShapeDtypeStruct((rows, d), jnp.float32),
        scratch_shapes=[
            pltpu.VMEM((tm, d), jnp.bfloat16),
            pltpu.VMEM((n_in, tm, tc), jnp.float32),
            pltpu.VMEM((tm, width), jnp.bfloat16),
        ],
        compiler_params=pltpu.CompilerParams(
            dimension_semantics=("arbitrary", "arbitrary"),
            vmem_limit_bytes=VMEM_LIMIT_BYTES),
        name="spatial_gate_mixer",
    )(x, g.reshape(1, d), col_blocks(w_in, n_in), v_norm.reshape(1, width), w_s, b_s.T,
      col_blocks(w_out, n_out))


def kernel(x, a_norm, a_in, a_conv, a_out, b_norm, b_in, b_vnorm, b_ws, b_bs, b_out,
           f_norm, f_up, f_conv_w, f_conv_b, f_down, final_norm):
    batch, seq, d = x.shape
    depth = f_norm.shape[0]
    assert seq % ROW_TILE == 0 and ROW_TILE % CHUNK == 0
    h = x.reshape(batch * seq, d)
    for layer in range(depth):
        m = layer // 2
        if layer % 2 == 0:
            h = _short_conv_block(h, a_norm[m], a_in[m], a_conv[m], a_out[m], seq=seq)
        else:
            h = _spatial_gate_block(h, b_norm[m], b_in[m], b_vnorm[m], b_ws[m], b_bs[m], b_out[m])
        h = _conv_ffn_block(h, f_norm[layer], f_up[layer], f_conv_w[layer], f_conv_b[layer],
                            f_down[layer], final_norm, seq=seq, final_norm=(layer == depth - 1))
    return h.reshape(batch, seq, d)
```

```python
import functools

import jax
import jax.numpy as jnp
from jax import lax
from jax.experimental import pallas as pl
from jax.experimental.pallas import tpu as pltpu

RMS_EPS = 1e-5
CHUNK = 128
SUBLANES = 8
LANES = 128
FFN_ROWS = 1024
FFN_TILE = 256
CONV_ROWS = 1024
CONV_TILE = 256
SG_ROWS = 512
SG_TILE = 1024
GATE_ROWS = 16
VMEM_LIMIT_BYTES = 56 * 1024 * 1024


def _rmsnorm(x, g):
    inv = lax.rsqrt(jnp.mean(x * x, axis=-1, keepdims=True) + RMS_EPS)
    return (x * inv) * g


def _bf16_dot(a, b):
    return jnp.dot(a, b, preferred_element_type=jnp.float32)


def _three_stage_step(j, nj, first, project, gate, contract, last):
    def step(parity, do_project, do_gate, do_contract):
        if do_gate:
            gate(1 - parity)
        if do_project:
            project(parity)
        if do_contract:
            contract(parity)

    @pl.when(j == 0)
    def _():
        first()
        step(0, True, False, False)

    @pl.when(j == 1)
    def _():
        step(1, True, True, False)

    for parity in range(2):
        @pl.when((j >= 2) & (j < nj) & (j % 2 == parity))
        def _(parity=parity):
            step(parity, True, True, True)

    @pl.when(j == nj)
    def _():
        step(nj % 2, False, True, True)

    @pl.when(j == nj + 1)
    def _():
        step((nj + 1) % 2, False, False, True)
        last()


def _stage_block(j, lag, nj):
    return jnp.clip(j - lag, 0, nj - 1)


def _swap_halo(raw, tail, cols, tm):
    raw[:SUBLANES, cols] = tail[...]
    tail[...] = raw[tm:, cols]


def _conv3_rows(raw, r0, rb, cols, w):
    return (raw[r0 - 2:r0 - 2 + rb, cols] * w[0:1] + raw[r0 - 1:r0 - 1 + rb, cols] * w[1:2]
            + raw[r0:r0 + rb, cols] * w[2:3])


def _short_conv_kernel(x_ref, g_ref, wb_ref, wc_ref, wx_ref, cw_ref, wout_ref, o_ref,
                       h_ref, raw_even, raw_odd, act_even, act_odd, carry_ref,
                       *, tiles_per_seq, tn, nj):
    i, j = pl.program_id(0), pl.program_id(1)
    raws, acts = (raw_even, raw_odd), (act_even, act_odd)
    tm = h_ref.shape[0]
    p_cols = slice(tn, 2 * tn)

    def first():
        x = x_ref[...]
        h_ref[...] = _rmsnorm(x, g_ref[...]).astype(h_ref.dtype)
        o_ref[...] = x

        @pl.when(i % tiles_per_seq == 0)
        def _():
            carry_ref[...] = jnp.zeros(carry_ref.shape, carry_ref.dtype)

    def project(parity):
        raw, h = raws[parity], h_ref[...]
        raw[SUBLANES:, :tn] = _bf16_dot(h, wb_ref[...])
        raw[SUBLANES:, p_cols] = _bf16_dot(h, wc_ref[...]) * _bf16_dot(h, wx_ref[...])

    def gate(parity):
        raw, act = raws[parity], acts[parity]
        _swap_halo(raw, carry_ref.at[j - 1], p_cols, tm)
        for c in range(tn // LANES):
            cs = slice(c * LANES, (c + 1) * LANES)
            w = cw_ref[:, cs]
            for r in range(tm // GATE_ROWS):
                r0 = SUBLANES + r * GATE_ROWS
                conv = _conv3_rows(raw, r0, GATE_ROWS, slice(tn + c * LANES, tn + (c + 1) * LANES), w)
                y = raw[r0:r0 + GATE_ROWS, cs] * conv
                act[r * GATE_ROWS:(r + 1) * GATE_ROWS, cs] = y.astype(act.dtype)

    def contract(parity):
        o_ref[...] += _bf16_dot(acts[parity][...], wout_ref[...])

    _three_stage_step(j, nj, first, project, gate, contract, lambda: None)


def _short_conv_block(x, g, w_in, w_conv, w_out, *, seq):
    rows, d = x.shape
    tm, tn = CONV_ROWS, CONV_TILE
    nj = d // tn
    win = w_in.astype(jnp.bfloat16)
    body = functools.partial(_short_conv_kernel, tiles_per_seq=seq // tm, tn=tn, nj=nj)
    return pl.pallas_call(
        body,
        grid=(rows // tm, nj + 2),
        in_specs=[
            pl.BlockSpec((tm, d), lambda i, j: (i, 0)),
            pl.BlockSpec((1, d), lambda i, j: (0, 0)),
            pl.BlockSpec((d, tn), lambda i, j: (0, _stage_block(j, 0, nj))),
            pl.BlockSpec((d, tn), lambda i, j: (0, nj + _stage_block(j, 0, nj))),
            pl.BlockSpec((d, tn), lambda i, j: (0, 2 * nj + _stage_block(j, 0, nj))),
            pl.BlockSpec((3, tn), lambda i, j: (0, _stage_block(j, 1, nj))),
            pl.BlockSpec((tn, d), lambda i, j: (_stage_block(j, 2, nj), 0)),
        ],
        out_specs=pl.BlockSpec((tm, d), lambda i, j: (i, 0)),
        out_shape=jax.ShapeDtypeStruct((rows, d), jnp.float32),
        scratch_shapes=[
            pltpu.VMEM((tm, d), jnp.bfloat16),
            pltpu.VMEM((tm + SUBLANES, 2 * tn), jnp.float32),
            pltpu.VMEM((tm + SUBLANES, 2 * tn), jnp.float32),
            pltpu.VMEM((tm, tn), jnp.bfloat16),
            pltpu.VMEM((tm, tn), jnp.bfloat16),
            pltpu.VMEM((nj, SUBLANES, tn), jnp.float32),
        ],
        compiler_params=pltpu.CompilerParams(
            dimension_semantics=("arbitrary", "arbitrary"),
            vmem_limit_bytes=VMEM_LIMIT_BYTES),
        name="short_conv_mixer",
    )(x, g.reshape(1, d), win, win, win, w_conv, w_out.astype(jnp.bfloat16))


def _conv_ffn_kernel(x_ref, g_ref, wg_ref, wa_ref, cwg_ref, cwa_ref, cbg_ref, cba_ref, wdown_ref,
                     fg_ref, o_ref, h_ref, raw_even, raw_odd, act_even, act_odd, carry_ref,
                     *, tiles_per_seq, tf, nj, final_norm):
    i, j = pl.program_id(0), pl.program_id(1)
    raws, acts = (raw_even, raw_odd), (act_even, act_odd)
    tm = h_ref.shape[0]

    def first():
        x = x_ref[...]
        h_ref[...] = _rmsnorm(x, g_ref[...]).astype(h_ref.dtype)
        o_ref[...] = x

        @pl.when(i % tiles_per_seq == 0)
        def _():
            carry_ref[...] = jnp.zeros(carry_ref.shape, carry_ref.dtype)

    def project(parity):
        raw, h = raws[parity], h_ref[...]
        raw[SUBLANES:, :tf] = _bf16_dot(h, wg_ref[...])
        raw[SUBLANES:, tf:] = _bf16_dot(h, wa_ref[...])

    def gate(parity):
        raw, act = raws[parity], acts[parity]
        _swap_halo(raw, carry_ref.at[j - 1], slice(None), tm)
        for c in range(tf // LANES):
            cs = slice(c * LANES, (c + 1) * LANES)
            wts = [(cw_ref[:, cs], cb_ref[:, cs]) for cw_ref, cb_ref in ((cwg_ref, cbg_ref), (cwa_ref, cba_ref))]
            for r in range(tm // GATE_ROWS):
                r0 = SUBLANES + r * GATE_ROWS
                ga = [_conv3_rows(raw, r0, GATE_ROWS, slice(p * tf + c * LANES, p * tf + (c + 1) * LANES), w) + b
                      for p, (w, b) in enumerate(wts)]
                act[r * GATE_ROWS:(r + 1) * GATE_ROWS, cs] = (jax.nn.silu(ga[0]) * ga[1]).astype(act.dtype)

    def contract(parity):
        o_ref[...] += _bf16_dot(acts[parity][...], wdown_ref[...])

    def last():
        if final_norm:
            o_ref[...] = _rmsnorm(o_ref[...], fg_ref[...])

    _three_stage_step(j, nj, first, project, gate, contract, last)


def _conv_ffn_block(x, g, w_up, conv_w, conv_b, w_down, final_g, *, seq, final_norm):
    rows, d = x.shape
    d_ff = w_down.shape[0]
    tm, tf = FFN_ROWS, FFN_TILE
    nj = d_ff // tf
    wup = w_up.astype(jnp.bfloat16)
    cb = conv_b.reshape(1, 2 * d_ff)
    body = functools.partial(_conv_ffn_kernel, tiles_per_seq=seq // tm, tf=tf, nj=nj,
                             final_norm=final_norm)
    return pl.pallas_call(
        body,
        grid=(rows // tm, nj + 2),
        in_specs=[
            pl.BlockSpec((tm, d), lambda i, j: (i, 0)),
            pl.BlockSpec((1, d), lambda i, j: (0, 0)),
            pl.BlockSpec((d, tf), lambda i, j: (0, _stage_block(j, 0, nj))),
            pl.BlockSpec((d, tf), lambda i, j: (0, nj + _stage_block(j, 0, nj))),
            pl.BlockSpec((3, tf), lambda i, j: (0, _stage_block(j, 1, nj))),
            pl.BlockSpec((3, tf), lambda i, j: (0, nj + _stage_block(j, 1, nj))),
            pl.BlockSpec((1, tf), lambda i, j: (0, _stage_block(j, 1, nj))),
            pl.BlockSpec((1, tf), lambda i, j: (0, nj + _stage_block(j, 1, nj))),
            pl.BlockSpec((tf, d), lambda i, j: (_stage_block(j, 2, nj), 0)),
            pl.BlockSpec((1, d), lambda i, j: (0, 0)),
        ],
        out_specs=pl.BlockSpec((tm, d), lambda i, j: (i, 0)),
        out_shape=jax.ShapeDtypeStruct((rows, d), jnp.float32),
        scratch_shapes=[
            pltpu.VMEM((tm, d), jnp.bfloat16),
            pltpu.VMEM((tm + SUBLANES, 2 * tf), jnp.float32),
            pltpu.VMEM((tm + SUBLANES, 2 * tf), jnp.float32),
            pltpu.VMEM((tm, tf), jnp.bfloat16),
            pltpu.VMEM((tm, tf), jnp.bfloat16),
            pltpu.VMEM((nj, SUBLANES, 2 * tf), jnp.float32),
        ],
        compiler_params=pltpu.CompilerParams(
            dimension_semantics=("arbitrary", "arbitrary"),
            vmem_limit_bytes=VMEM_LIMIT_BYTES),
        name="conv_ffn",
    )(x, g.reshape(1, d), wup, wup, conv_w, conv_w, cb, cb, w_down.astype(jnp.bfloat16),
      final_g.reshape(1, d))


def _spatial_gate_kernel(x_ref, g_ref, win_ref, vg_ref, ws_ref, bst_ref, wout_ref, o_ref,
                         h_ref, uv_ref, ug_ref, *, groups):
    n_in, tm, tc = uv_ref.shape
    width = wout_ref.shape[0]
    half_blocks = n_in // 2
    gw = width // groups
    groups_per_block = tc // gw

    h_ref[...] = _rmsnorm(x_ref[...], g_ref[...]).astype(h_ref.dtype)

    def project(c):
        uv_ref[c] = _bf16_dot(h_ref[...], win_ref[:, c * tc:(c + 1) * tc])

    def gelu(c):
        uv_ref[c] = jax.nn.gelu(uv_ref[c])

    project(0)
    for c in range(1, n_in):
        gelu(c - 1)
        project(c)
    gelu(n_in - 1)

    ssq = jnp.zeros((tm, 1), jnp.float32)
    for b in range(half_blocks):
        v = uv_ref[half_blocks + b]
        ssq += jnp.sum(v * v, axis=-1, keepdims=True)
    inv = lax.rsqrt(ssq / width + RMS_EPS)
    row = lax.broadcasted_iota(jnp.int32, (CHUNK, CHUNK), 0)
    col = lax.broadcasted_iota(jnp.int32, (CHUNK, CHUNK), 1)
    tril = (row >= col).astype(jnp.float32)
    for b in range(half_blocks):
        vn = ((uv_ref[half_blocks + b] * inv) * vg_ref[:, b * tc:(b + 1) * tc]).astype(jnp.bfloat16)
        for gl in range(groups_per_block):
            hg = b * groups_per_block + gl
            ws = (ws_ref[hg] * tril).astype(jnp.bfloat16)
            bias = bst_ref[:, hg:hg + 1]
            for c in range(tm // CHUNK):
                rs = slice(c * CHUNK, (c + 1) * CHUNK)
                gate = _bf16_dot(ws, vn[rs, gl * gw:(gl + 1) * gw]) + bias
                u = uv_ref[b, rs, gl * gw:(gl + 1) * gw]
                ug_ref[rs, hg * gw:(hg + 1) * gw] = (u * gate).astype(ug_ref.dtype)

    for k in range(o_ref.shape[1] // tc):
        cs = slice(k * tc, (k + 1) * tc)
        o_ref[:, cs] = x_ref[:, cs] + _bf16_dot(ug_ref[...], wout_ref[:, cs])


def _spatial_gate_block(x, g, w_in, v_norm, w_s, b_s, w_out):
    rows, d = x.shape
    width = w_out.shape[0]
    groups = w_s.shape[0]
    tm, tc = SG_ROWS, SG_TILE
    resident = dict(pipeline_mode=pl.Buffered(1))
    body = functools.partial(_spatial_gate_kernel, groups=groups)
    return pl.pallas_call(
        body,
        grid=(rows // tm,),
        in_specs=[
            pl.BlockSpec((tm, d), lambda i: (i, 0)),
            pl.BlockSpec((1, d), lambda i: (0, 0)),
            pl.BlockSpec((d, 2 * width), lambda i: (0, 0), **resident),
            pl.BlockSpec((1, width), lambda i: (0, 0)),
            pl.BlockSpec((groups, CHUNK, CHUNK), lambda i: (0, 0, 0)),
            pl.BlockSpec((CHUNK, groups), lambda i: (0, 0)),
            pl.BlockSpec((width, d), lambda i: (0, 0), **resident),
        ],
        out_specs=pl.BlockSpec((tm, d), lambda i: (i, 0)),
        out_shape=jax.ShapeDtypeStruct((rows, d), jnp.float32),
        scratch_shapes=[
            pltpu.VMEM((tm, d), jnp.bfloat16),
            pltpu.VMEM((2 * width // tc, tm, tc), jnp.float32),
            pltpu.VMEM((tm, width), jnp.bfloat16),
        ],
        compiler_params=pltpu.CompilerParams(
            dimension_semantics=("arbitrary",),
            vmem_limit_bytes=VMEM_LIMIT_BYTES),
        name="spatial_gate_mixer",
    )(x, g.reshape(1, d), w_in.astype(jnp.bfloat16), v_norm.reshape(1, width), w_s, b_s.T,
      w_out.astype(jnp.bfloat16))


def kernel(x, a_norm, a_in, a_conv, a_out, b_norm, b_in, b_vnorm, b_ws, b_bs, b_out,
           f_norm, f_up, f_conv_w, f_conv_b, f_down, final_norm):
    batch, seq, d = x.shape
    depth = f_norm.shape[0]
    assert seq % FFN_ROWS == 0 and seq % CONV_ROWS == 0 and seq % SG_ROWS == 0 and SG_ROWS % CHUNK == 0
    h = x.reshape(batch * seq, d)
    for layer in range(depth):
        m = layer // 2
        if layer % 2 == 0:
            h = _short_conv_block(h, a_norm[m], a_in[m], a_conv[m], a_out[m], seq=seq)
        else:
            h = _spatial_gate_block(h, b_norm[m], b_in[m], b_vnorm[m], b_ws[m], b_bs[m], b_out[m])
        h = _conv_ffn_block(h, f_norm[layer], f_up[layer], f_conv_w[layer], f_conv_b[layer],
                            f_down[layer], final_norm, seq=seq, final_norm=(layer == depth - 1))
    return h.reshape(batch, seq, d)
```

```python
import functools

import jax
import jax.numpy as jnp
from jax import lax
from jax.experimental import pallas as pl
from jax.experimental.pallas import tpu as pltpu

RMS_EPS = 1e-5
CHUNK = 128
SUBLANES = 8
LANES = 128
FFN_ROWS = 1024
FFN_TILE = 256
CONV_ROWS = 1024
CONV_TILE = 256
SG_ROWS = 512
SG_TILE = 1024
GATE_ROWS = 16
ROW_SPLIT = 4
VMEM_LIMIT_BYTES = 56 * 1024 * 1024


def _rmsnorm(x, g):
    inv = lax.rsqrt(jnp.mean(x * x, axis=-1, keepdims=True) + RMS_EPS)
    return (x * inv) * g


def _bf16_dot(a, b):
    return jnp.dot(a, b, preferred_element_type=jnp.float32)


def _slab_dot(h_ref, w_refs):
    kb = w_refs[0].shape[0]
    acc = None
    for r, w_ref in enumerate(w_refs):
        part = _bf16_dot(h_ref[:, r * kb:(r + 1) * kb], w_ref[...])
        acc = part if acc is None else acc + part
    return acc


def _slab_specs(rows, cols, col_block):
    kb = rows // ROW_SPLIT
    return [pl.BlockSpec((kb, cols), functools.partial(lambda r, i, j: (r, col_block(j)), r))
            for r in range(ROW_SPLIT)]


def _three_stage_step(j, nj, first, project, gate, contract, last):
    def step(parity, do_project, do_gate, do_contract):
        if do_gate:
            gate(1 - parity)
        if do_project:
            project(parity)
        if do_contract:
            contract(parity)

    @pl.when(j == 0)
    def _():
        first()
        step(0, True, False, False)

    @pl.when(j == 1)
    def _():
        step(1, True, True, False)

    for parity in range(2):
        @pl.when((j >= 2) & (j < nj) & (j % 2 == parity))
        def _(parity=parity):
            step(parity, True, True, True)

    @pl.when(j == nj)
    def _():
        step(nj % 2, False, True, True)

    @pl.when(j == nj + 1)
    def _():
        step((nj + 1) % 2, False, False, True)
        last()


def _stage_block(j, lag, nj):
    return jnp.clip(j - lag, 0, nj - 1)


def _swap_halo(raw, tail, cols, tm):
    raw[:SUBLANES, cols] = tail[...]
    tail[...] = raw[tm:, cols]


def _conv3_rows(raw, r0, rb, cols, w):
    return (raw[r0 - 2:r0 - 2 + rb, cols] * w[0:1] + raw[r0 - 1:r0 - 1 + rb, cols] * w[1:2]
            + raw[r0:r0 + rb, cols] * w[2:3])


def _short_conv_kernel(x_ref, g_ref, *refs, tiles_per_seq, tn, nj):
    wb_refs, wc_refs, wx_refs = (refs[k * ROW_SPLIT:(k + 1) * ROW_SPLIT] for k in range(3))
    (cw_ref, wout_ref, o_ref, h_ref, raw_even, raw_odd, act_even, act_odd,
     carry_ref) = refs[3 * ROW_SPLIT:]
    i, j = pl.program_id(0), pl.program_id(1)
    raws, acts = (raw_even, raw_odd), (act_even, act_odd)
    tm = h_ref.shape[0]
    p_cols = slice(tn, 2 * tn)

    def first():
        x = x_ref[...]
        h_ref[...] = _rmsnorm(x, g_ref[...]).astype(h_ref.dtype)
        o_ref[...] = x

        @pl.when(i % tiles_per_seq == 0)
        def _():
            carry_ref[...] = jnp.zeros(carry_ref.shape, carry_ref.dtype)

    def project(parity):
        raw = raws[parity]
        raw[SUBLANES:, :tn] = _slab_dot(h_ref, wb_refs)
        raw[SUBLANES:, p_cols] = _slab_dot(h_ref, wc_refs) * _slab_dot(h_ref, wx_refs)

    def gate(parity):
        raw, act = raws[parity], acts[parity]
        _swap_halo(raw, carry_ref.at[j - 1], p_cols, tm)
        for c in range(tn // LANES):
            cs = slice(c * LANES, (c + 1) * LANES)
            w = cw_ref[:, cs]
            for r in range(tm // GATE_ROWS):
                r0 = SUBLANES + r * GATE_ROWS
                conv = _conv3_rows(raw, r0, GATE_ROWS, slice(tn + c * LANES, tn + (c + 1) * LANES), w)
                y = raw[r0:r0 + GATE_ROWS, cs] * conv
                act[r * GATE_ROWS:(r + 1) * GATE_ROWS, cs] = y.astype(act.dtype)

    def contract(parity):
        o_ref[...] += _bf16_dot(acts[parity][...], wout_ref[...])

    _three_stage_step(j, nj, first, project, gate, contract, lambda: None)


def _short_conv_block(x, g, w_in, w_conv, w_out, *, seq):
    rows, d = x.shape
    tm, tn = CONV_ROWS, CONV_TILE
    nj = d // tn
    win = w_in.astype(jnp.bfloat16)
    body = functools.partial(_short_conv_kernel, tiles_per_seq=seq // tm, tn=tn, nj=nj)
    return pl.pallas_call(
        body,
        grid=(rows // tm, nj + 2),
        in_specs=[
            pl.BlockSpec((tm, d), lambda i, j: (i, 0)),
            pl.BlockSpec((1, d), lambda i, j: (0, 0)),
            *_slab_specs(d, tn, lambda j: _stage_block(j, 0, nj)),
            *_slab_specs(d, tn, lambda j: nj + _stage_block(j, 0, nj)),
            *_slab_specs(d, tn, lambda j: 2 * nj + _stage_block(j, 0, nj)),
            pl.BlockSpec((3, tn), lambda i, j: (0, _stage_block(j, 1, nj))),
            pl.BlockSpec((tn, d), lambda i, j: (_stage_block(j, 2, nj), 0)),
        ],
        out_specs=pl.BlockSpec((tm, d), lambda i, j: (i, 0)),
        out_shape=jax.ShapeDtypeStruct((rows, d), jnp.float32),
        scratch_shapes=[
            pltpu.VMEM((tm, d), jnp.bfloat16),
            pltpu.VMEM((tm + SUBLANES, 2 * tn), jnp.float32),
            pltpu.VMEM((tm + SUBLANES, 2 * tn), jnp.float32),
            pltpu.VMEM((tm, tn), jnp.bfloat16),
            pltpu.VMEM((tm, tn), jnp.bfloat16),
            pltpu.VMEM((nj, SUBLANES, tn), jnp.float32),
        ],
        compiler_params=pltpu.CompilerParams(
            dimension_semantics=("arbitrary", "arbitrary"),
            vmem_limit_bytes=VMEM_LIMIT_BYTES),
        name="short_conv_mixer",
    )(x, g.reshape(1, d), *([win] * (3 * ROW_SPLIT)), w_conv, w_out.astype(jnp.bfloat16))


def _conv_ffn_kernel(x_ref, g_ref, *refs, tiles_per_seq, tf, nj, final_norm):
    wg_refs, wa_refs = refs[:ROW_SPLIT], refs[ROW_SPLIT:2 * ROW_SPLIT]
    (cwg_ref, cwa_ref, cbg_ref, cba_ref, wdown_ref, fg_ref, o_ref, h_ref, raw_even, raw_odd,
     act_even, act_odd, carry_ref) = refs[2 * ROW_SPLIT:]
    i, j = pl.program_id(0), pl.program_id(1)
    raws, acts = (raw_even, raw_odd), (act_even, act_odd)
    tm = h_ref.shape[0]

    def first():
        x = x_ref[...]
        h_ref[...] = _rmsnorm(x, g_ref[...]).astype(h_ref.dtype)
        o_ref[...] = x

        @pl.when(i % tiles_per_seq == 0)
        def _():
            carry_ref[...] = jnp.zeros(carry_ref.shape, carry_ref.dtype)

    def project(parity):
        raw = raws[parity]
        raw[SUBLANES:, :tf] = _slab_dot(h_ref, wg_refs)
        raw[SUBLANES:, tf:] = _slab_dot(h_ref, wa_refs)

    def gate(parity):
        raw, act = raws[parity], acts[parity]
        _swap_halo(raw, carry_ref.at[j - 1], slice(None), tm)
        for c in range(tf // LANES):
            cs = slice(c * LANES, (c + 1) * LANES)
            wts = [(cw_ref[:, cs], cb_ref[:, cs]) for cw_ref, cb_ref in ((cwg_ref, cbg_ref), (cwa_ref, cba_ref))]
            for r in range(tm // GATE_ROWS):
                r0 = SUBLANES + r * GATE_ROWS
                ga = [_conv3_rows(raw, r0, GATE_ROWS, slice(p * tf + c * LANES, p * tf + (c + 1) * LANES), w) + b
                      for p, (w, b) in enumerate(wts)]
                act[r * GATE_ROWS:(r + 1) * GATE_ROWS, cs] = (jax.nn.silu(ga[0]) * ga[1]).astype(act.dtype)

    def contract(parity):
        o_ref[...] += _bf16_dot(acts[parity][...], wdown_ref[...])

    def last():
        if final_norm:
            o_ref[...] = _rmsnorm(o_ref[...], fg_ref[...])

    _three_stage_step(j, nj, first, project, gate, contract, last)


def _conv_ffn_block(x, g, w_up, conv_w, conv_b, w_down, final_g, *, seq, final_norm):
    rows, d = x.shape
    d_ff = w_down.shape[0]
    tm, tf = FFN_ROWS, FFN_TILE
    nj = d_ff // tf
    wup = w_up.astype(jnp.bfloat16)
    cb = conv_b.reshape(1, 2 * d_ff)
    body = functools.partial(_conv_ffn_kernel, tiles_per_seq=seq // tm, tf=tf, nj=nj,
                             final_norm=final_norm)
    return pl.pallas_call(
        body,
        grid=(rows // tm, nj + 2),
        in_specs=[
            pl.BlockSpec((tm, d), lambda i, j: (i, 0)),
            pl.BlockSpec((1, d), lambda i, j: (0, 0)),
            *_slab_specs(d, tf, lambda j: _stage_block(j, 0, nj)),
            *_slab_specs(d, tf, lambda j: nj + _stage_block(j, 0, nj)),
            pl.BlockSpec((3, tf), lambda i, j: (0, _stage_block(j, 1, nj))),
            pl.BlockSpec((3, tf), lambda i, j: (0, nj + _stage_block(j, 1, nj))),
            pl.BlockSpec((1, tf), lambda i, j: (0, _stage_block(j, 1, nj))),
            pl.BlockSpec((1, tf), lambda i, j: (0, nj + _stage_block(j, 1, nj))),
            pl.BlockSpec((tf, d), lambda i, j: (_stage_block(j, 2, nj), 0)),
            pl.BlockSpec((1, d), lambda i, j: (0, 0)),
        ],
        out_specs=pl.BlockSpec((tm, d), lambda i, j: (i, 0)),
        out_shape=jax.ShapeDtypeStruct((rows, d), jnp.float32),
        scratch_shapes=[
            pltpu.VMEM((tm, d), jnp.bfloat16),
            pltpu.VMEM((tm + SUBLANES, 2 * tf), jnp.float32),
            pltpu.VMEM((tm + SUBLANES, 2 * tf), jnp.float32),
            pltpu.VMEM((tm, tf), jnp.bfloat16),
            pltpu.VMEM((tm, tf), jnp.bfloat16),
            pltpu.VMEM((nj, SUBLANES, 2 * tf), jnp.float32),
        ],
        compiler_params=pltpu.CompilerParams(
            dimension_semantics=("arbitrary", "arbitrary"),
            vmem_limit_bytes=VMEM_LIMIT_BYTES),
        name="conv_ffn",
    )(x, g.reshape(1, d), *([wup] * (2 * ROW_SPLIT)), conv_w, conv_w, cb, cb,
      w_down.astype(jnp.bfloat16), final_g.reshape(1, d))


def _spatial_gate_kernel(x_ref, g_ref, win_ref, vg_ref, ws_ref, bst_ref, wout_ref, o_ref,
                         h_ref, uv_ref, ug_ref, *, groups):
    n_in, tm, tc = uv_ref.shape
    width = wout_ref.shape[0]
    half_blocks = n_in // 2
    gw = width // groups
    groups_per_block = tc // gw

    h_ref[...] = _rmsnorm(x_ref[...], g_ref[...]).astype(h_ref.dtype)

    def project(c):
        uv_ref[c] = _bf16_dot(h_ref[...], win_ref[:, c * tc:(c + 1) * tc])

    def gelu(c):
        uv_ref[c] = jax.nn.gelu(uv_ref[c])

    project(0)
    for c in range(1, n_in):
        gelu(c - 1)
        project(c)
    gelu(n_in - 1)

    ssq = jnp.zeros((tm, 1), jnp.float32)
    for b in range(half_blocks):
        v = uv_ref[half_blocks + b]
        ssq += jnp.sum(v * v, axis=-1, keepdims=True)
    inv = lax.rsqrt(ssq / width + RMS_EPS)
    row = lax.broadcasted_iota(jnp.int32, (CHUNK, CHUNK), 0)
    col = lax.broadcasted_iota(jnp.int32, (CHUNK, CHUNK), 1)
    tril = (row >= col).astype(jnp.float32)
    for b in range(half_blocks):
        vn = ((uv_ref[half_blocks + b] * inv) * vg_ref[:, b * tc:(b + 1) * tc]).astype(jnp.bfloat16)
        for gl in range(groups_per_block):
            hg = b * groups_per_block + gl
            ws = (ws_ref[hg] * tril).astype(jnp.bfloat16)
            bias = bst_ref[:, hg:hg + 1]
            for c in range(tm // CHUNK):
                rs = slice(c * CHUNK, (c + 1) * CHUNK)
                gate = _bf16_dot(ws, vn[rs, gl * gw:(gl + 1) * gw]) + bias
                u = uv_ref[b, rs, gl * gw:(gl + 1) * gw]
                ug_ref[rs, hg * gw:(hg + 1) * gw] = (u * gate).astype(ug_ref.dtype)

    for k in range(o_ref.shape[1] // tc):
        cs = slice(k * tc, (k + 1) * tc)
        o_ref[:, cs] = x_ref[:, cs] + _bf16_dot(ug_ref[...], wout_ref[:, cs])


def _spatial_gate_block(x, g, w_in, v_norm, w_s, b_s, w_out):
    rows, d = x.shape
    width = w_out.shape[0]
    groups = w_s.shape[0]
    tm, tc = SG_ROWS, SG_TILE
    resident = dict(pipeline_mode=pl.Buffered(1))
    body = functools.partial(_spatial_gate_kernel, groups=groups)
    return pl.pallas_call(
        body,
        grid=(rows // tm,),
        in_specs=[
            pl.BlockSpec((tm, d), lambda i: (i, 0)),
            pl.BlockSpec((1, d), lambda i: (0, 0)),
            pl.BlockSpec((d, 2 * width), lambda i: (0, 0), **resident),
            pl.BlockSpec((1, width), lambda i: (0, 0)),
            pl.BlockSpec((groups, CHUNK, CHUNK), lambda i: (0, 0, 0)),
            pl.BlockSpec((CHUNK, groups), lambda i: (0, 0)),
            pl.BlockSpec((width, d), lambda i: (0, 0), **resident),
        ],
        out_specs=pl.BlockSpec((tm, d), lambda i: (i, 0)),
        out_shape=jax.ShapeDtypeStruct((rows, d), jnp.float32),
        scratch_shapes=[
            pltpu.VMEM((tm, d), jnp.bfloat16),
            pltpu.VMEM((2 * width // tc, tm, tc), jnp.float32),
            pltpu.VMEM((tm, width), jnp.bfloat16),
        ],
        compiler_params=pltpu.CompilerParams(
            dimension_semantics=("arbitrary",),
            vmem_limit_bytes=VMEM_LIMIT_BYTES),
        name="spatial_gate_mixer",
    )(x, g.reshape(1, d), w_in.astype(jnp.bfloat16), v_norm.reshape(1, width), w_s, b_s.T,
      w_out.astype(jnp.bfloat16))


def kernel(x, a_norm, a_in, a_conv, a_out, b_norm, b_in, b_vnorm, b_ws, b_bs, b_out,
           f_norm, f_up, f_conv_w, f_conv_b, f_down, final_norm):
    batch, seq, d = x.shape
    depth = f_norm.shape[0]
    assert seq % FFN_ROWS == 0 and seq % CONV_ROWS == 0 and seq % SG_ROWS == 0 and SG_ROWS % CHUNK == 0
    h = x.reshape(batch * seq, d)
    for layer in range(depth):
        m = layer // 2
        if layer % 2 == 0:
            h = _short_conv_block(h, a_norm[m], a_in[m], a_conv[m], a_out[m], seq=seq)
        else:
            h = _spatial_gate_block(h, b_norm[m], b_in[m], b_vnorm[m], b_ws[m], b_bs[m], b_out[m])
        h = _conv_ffn_block(h, f_norm[layer], f_up[layer], f_conv_w[layer], f_conv_b[layer],
                            f_down[layer], final_norm, seq=seq, final_norm=(layer == depth - 1))
    return h.reshape(batch, seq, d)
```

```python
import functools

import jax
import jax.numpy as jnp
from jax import lax
from jax.experimental import pallas as pl
from jax.experimental.pallas import tpu as pltpu

RMS_EPS = 1e-5
CHUNK = 128
SUBLANES = 8
LANES = 128
FFN_ROWS = 1024
FFN_TILE = 256
CONV_ROWS = 1024
CONV_TILE = 256
SG_ROWS = 512
SG_TILE = 1024
GATE_ROWS = 16
ROW_SPLIT = 4
GATE_CHAINS = 2
VMEM_LIMIT_BYTES = 56 * 1024 * 1024


def _rmsnorm(x, g):
    inv = lax.rsqrt(jnp.mean(x * x, axis=-1, keepdims=True) + RMS_EPS)
    return (x * inv) * g


def _bf16_dot(a, b):
    return jnp.dot(a, b, preferred_element_type=jnp.float32)


def _slab_dot(h_ref, w_refs):
    kb = w_refs[0].shape[0]
    acc = None
    for r, w_ref in enumerate(w_refs):
        part = _bf16_dot(h_ref[:, r * kb:(r + 1) * kb], w_ref[...])
        acc = part if acc is None else acc + part
    return acc


def _slab_specs(rows, cols, col_block):
    kb = rows // ROW_SPLIT
    return [pl.BlockSpec((kb, cols), functools.partial(lambda r, i, j: (r, col_block(j)), r))
            for r in range(ROW_SPLIT)]


def _three_stage_step(j, nj, first, project, gate, contract, last):
    def step(parity, do_project, do_gate, do_contract):
        if do_gate:
            gate(1 - parity)
        if do_project:
            project(parity)
        if do_contract:
            contract(parity)

    @pl.when(j == 0)
    def _():
        first()
        step(0, True, False, False)

    @pl.when(j == 1)
    def _():
        step(1, True, True, False)

    for parity in range(2):
        @pl.when((j >= 2) & (j < nj) & (j % 2 == parity))
        def _(parity=parity):
            step(parity, True, True, True)

    @pl.when(j == nj)
    def _():
        step(nj % 2, False, True, True)

    @pl.when(j == nj + 1)
    def _():
        step((nj + 1) % 2, False, False, True)
        last()


def _stage_block(j, lag, nj):
    return jnp.clip(j - lag, 0, nj - 1)


def _swap_halo(raw, tail, cols, tm):
    raw[:SUBLANES, cols] = tail[...]
    tail[...] = raw[tm:, cols]


def _conv3_rows(raw, r0, rb, cols, w, token):
    win = _after(raw[r0 - SUBLANES:r0 + rb, cols], token)
    lo = SUBLANES
    return win[lo - 2:lo - 2 + rb] * w[0:1] + win[lo - 1:lo - 1 + rb] * w[1:2] + win[lo:] * w[2:3]


def _zero_bits():
    return jnp.zeros((SUBLANES, LANES), jnp.int32)


def _token(y, zero_bits):
    return lax.bitcast_convert_type(y[:SUBLANES], jnp.int32) & zero_bits


def _after(x, token):
    if token is None:
        return x
    bits = lax.bitcast_convert_type(x, jnp.int32) | jnp.concatenate([token] * (x.shape[0] // SUBLANES), axis=0)
    return lax.bitcast_convert_type(bits, jnp.float32)


def _short_conv_kernel(x_ref, g_ref, zero_ref, *refs, tiles_per_seq, tn, nj):
    wb_refs, wc_refs, wx_refs = (refs[k * ROW_SPLIT:(k + 1) * ROW_SPLIT] for k in range(3))
    (cw_ref, wout_ref, o_ref, h_ref, raw_even, raw_odd, act_even, act_odd,
     carry_ref) = refs[3 * ROW_SPLIT:]
    i, j = pl.program_id(0), pl.program_id(1)
    raws, acts = (raw_even, raw_odd), (act_even, act_odd)
    tm = h_ref.shape[0]
    p_cols = slice(tn, 2 * tn)

    def first():
        x = x_ref[...]
        h_ref[...] = _rmsnorm(x, g_ref[...]).astype(h_ref.dtype)
        o_ref[...] = x

        @pl.when(i % tiles_per_seq == 0)
        def _():
            carry_ref[...] = jnp.zeros(carry_ref.shape, carry_ref.dtype)

    def project(parity):
        raw = raws[parity]
        raw[SUBLANES:, :tn] = _slab_dot(h_ref, wb_refs)
        raw[SUBLANES:, p_cols] = _slab_dot(h_ref, wc_refs) * _slab_dot(h_ref, wx_refs)

    def gate(parity):
        raw, act = raws[parity], acts[parity]
        _swap_halo(raw, carry_ref.at[j - 1], p_cols, tm)
        zero_bits, tokens = zero_ref[...], [None] * GATE_CHAINS
        for c in range(tn // LANES):
            cs = slice(c * LANES, (c + 1) * LANES)
            w = cw_ref[:, cs]
            for r in range(tm // GATE_ROWS):
                r0, token = SUBLANES + r * GATE_ROWS, tokens[r % GATE_CHAINS]
                conv = _conv3_rows(raw, r0, GATE_ROWS, slice(tn + c * LANES, tn + (c + 1) * LANES), w, token)
                y = _after(raw[r0:r0 + GATE_ROWS, cs], token) * conv
                tokens[r % GATE_CHAINS] = _token(y, zero_bits)
                act[r * GATE_ROWS:(r + 1) * GATE_ROWS, cs] = y.astype(act.dtype)

    def contract(parity):
        o_ref[...] += _bf16_dot(acts[parity][...], wout_ref[...])

    _three_stage_step(j, nj, first, project, gate, contract, lambda: None)


def _short_conv_block(x, g, w_in, w_conv, w_out, *, seq):
    rows, d = x.shape
    tm, tn = CONV_ROWS, CONV_TILE
    nj = d // tn
    win = w_in.astype(jnp.bfloat16)
    body = functools.partial(_short_conv_kernel, tiles_per_seq=seq // tm, tn=tn, nj=nj)
    return pl.pallas_call(
        body,
        grid=(rows // tm, nj + 2),
        in_specs=[
            pl.BlockSpec((tm, d), lambda i, j: (i, 0)),
            pl.BlockSpec((1, d), lambda i, j: (0, 0)),
            pl.BlockSpec((SUBLANES, LANES), lambda i, j: (0, 0)),
            *_slab_specs(d, tn, lambda j: _stage_block(j, 0, nj)),
            *_slab_specs(d, tn, lambda j: nj + _stage_block(j, 0, nj)),
            *_slab_specs(d, tn, lambda j: 2 * nj + _stage_block(j, 0, nj)),
            pl.BlockSpec((3, tn), lambda i, j: (0, _stage_block(j, 1, nj))),
            pl.BlockSpec((tn, d), lambda i, j: (_stage_block(j, 2, nj), 0)),
        ],
        out_specs=pl.BlockSpec((tm, d), lambda i, j: (i, 0)),
        out_shape=jax.ShapeDtypeStruct((rows, d), jnp.float32),
        scratch_shapes=[
            pltpu.VMEM((tm, d), jnp.bfloat16),
            pltpu.VMEM((tm + SUBLANES, 2 * tn), jnp.float32),
            pltpu.VMEM((tm + SUBLANES, 2 * tn), jnp.float32),
            pltpu.VMEM((tm, tn), jnp.bfloat16),
            pltpu.VMEM((tm, tn), jnp.bfloat16),
            pltpu.VMEM((nj, SUBLANES, tn), jnp.float32),
        ],
        compiler_params=pltpu.CompilerParams(
            dimension_semantics=("arbitrary", "arbitrary"),
            vmem_limit_bytes=VMEM_LIMIT_BYTES),
        name="short_conv_mixer",
    )(x, g.reshape(1, d), _zero_bits(), *([win] * (3 * ROW_SPLIT)), w_conv, w_out.astype(jnp.bfloat16))


def _conv_ffn_kernel(x_ref, g_ref, zero_ref, *refs, tiles_per_seq, tf, nj, final_norm):
    wg_refs, wa_refs = refs[:ROW_SPLIT], refs[ROW_SPLIT:2 * ROW_SPLIT]
    (cwg_ref, cwa_ref, cbg_ref, cba_ref, wdown_ref, fg_ref, o_ref, h_ref, raw_even, raw_odd,
     act_even, act_odd, carry_ref) = refs[2 * ROW_SPLIT:]
    i, j = pl.program_id(0), pl.program_id(1)
    raws, acts = (raw_even, raw_odd), (act_even, act_odd)
    tm = h_ref.shape[0]

    def first():
        x = x_ref[...]
        h_ref[...] = _rmsnorm(x, g_ref[...]).astype(h_ref.dtype)
        o_ref[...] = x

        @pl.when(i % tiles_per_seq == 0)
        def _():
            carry_ref[...] = jnp.zeros(carry_ref.shape, carry_ref.dtype)

    def project(parity):
        raw = raws[parity]
        raw[SUBLANES:, :tf] = _slab_dot(h_ref, wg_refs)
        raw[SUBLANES:, tf:] = _slab_dot(h_ref, wa_refs)

    def gate(parity):
        raw, act = raws[parity], acts[parity]
        _swap_halo(raw, carry_ref.at[j - 1], slice(None), tm)
        zero_bits, tokens = zero_ref[...], [None] * GATE_CHAINS
        for c in range(tf // LANES):
            cs = slice(c * LANES, (c + 1) * LANES)
            wts = [(cw_ref[:, cs], cb_ref[:, cs]) for cw_ref, cb_ref in ((cwg_ref, cbg_ref), (cwa_ref, cba_ref))]
            for r in range(tm // GATE_ROWS):
                r0, token = SUBLANES + r * GATE_ROWS, tokens[r % GATE_CHAINS]
                ga = [_conv3_rows(raw, r0, GATE_ROWS, slice(p * tf + c * LANES, p * tf + (c + 1) * LANES), w, token) + b
                      for p, (w, b) in enumerate(wts)]
                y = jax.nn.silu(ga[0]) * ga[1]
                tokens[r % GATE_CHAINS] = _token(y, zero_bits)
                act[r * GATE_ROWS:(r + 1) * GATE_ROWS, cs] = y.astype(act.dtype)

    def contract(parity):
        o_ref[...] += _bf16_dot(acts[parity][...], wdown_ref[...])

    def last():
        if final_norm:
            o_ref[...] = _rmsnorm(o_ref[...], fg_ref[...])

    _three_stage_step(j, nj, first, project, gate, contract, last)


def _conv_ffn_block(x, g, w_up, conv_w, conv_b, w_down, final_g, *, seq, final_norm):
    rows, d = x.shape
    d_ff = w_down.shape[0]
    tm, tf = FFN_ROWS, FFN_TILE
    nj = d_ff // tf
    wup = w_up.astype(jnp.bfloat16)
    cb = conv_b.reshape(1, 2 * d_ff)
    body = functools.partial(_conv_ffn_kernel, tiles_per_seq=seq // tm, tf=tf, nj=nj,
                             final_norm=final_norm)
    return pl.pallas_call(
        body,
        grid=(rows // tm, nj + 2),
        in_specs=[
            pl.BlockSpec((tm, d), lambda i, j: (i, 0)),
            pl.BlockSpec((1, d), lambda i, j: (0, 0)),
            pl.BlockSpec((SUBLANES, LANES), lambda i, j: (0, 0)),
            *_slab_specs(d, tf, lambda j: _stage_block(j, 0, nj)),
            *_slab_specs(d, tf, lambda j: nj + _stage_block(j, 0, nj)),
            pl.BlockSpec((3, tf), lambda i, j: (0, _stage_block(j, 1, nj))),
            pl.BlockSpec((3, tf), lambda i, j: (0, nj + _stage_block(j, 1, nj))),
            pl.BlockSpec((1, tf), lambda i, j: (0, _stage_block(j, 1, nj))),
            pl.BlockSpec((1, tf), lambda i, j: (0, nj + _stage_block(j, 1, nj))),
            pl.BlockSpec((tf, d), lambda i, j: (_stage_block(j, 2, nj), 0)),
            pl.BlockSpec((1, d), lambda i, j: (0, 0)),
        ],
        out_specs=pl.BlockSpec((tm, d), lambda i, j: (i, 0)),
        out_shape=jax.ShapeDtypeStruct((rows, d), jnp.float32),
        scratch_shapes=[
            pltpu.VMEM((tm, d), jnp.bfloat16),
            pltpu.VMEM((tm + SUBLANES, 2 * tf), jnp.float32),
            pltpu.VMEM((tm + SUBLANES, 2 * tf), jnp.float32),
            pltpu.VMEM((tm, tf), jnp.bfloat16),
            pltpu.VMEM((tm, tf), jnp.bfloat16),
            pltpu.VMEM((nj, SUBLANES, 2 * tf), jnp.float32),
        ],
        compiler_params=pltpu.CompilerParams(
            dimension_semantics=("arbitrary", "arbitrary"),
            vmem_limit_bytes=VMEM_LIMIT_BYTES),
        name="conv_ffn",
    )(x, g.reshape(1, d), _zero_bits(), *([wup] * (2 * ROW_SPLIT)), conv_w, conv_w, cb, cb,
      w_down.astype(jnp.bfloat16), final_g.reshape(1, d))


def _spatial_gate_kernel(x_ref, g_ref, win_ref, vg_ref, ws_ref, bst_ref, wout_ref, o_ref,
                         h_ref, uv_ref, ug_ref, *, groups):
    n_in, tm, tc = uv_ref.shape
    width = wout_ref.shape[0]
    half_blocks = n_in // 2
    gw = width // groups
    groups_per_block = tc // gw

    h_ref[...] = _rmsnorm(x_ref[...], g_ref[...]).astype(h_ref.dtype)

    def project(c):
        uv_ref[c] = _bf16_dot(h_ref[...], win_ref[:, c * tc:(c + 1) * tc])

    def gelu(c):
        uv_ref[c] = jax.nn.gelu(uv_ref[c])

    project(0)
    for c in range(1, n_in):
        gelu(c - 1)
        project(c)
    gelu(n_in - 1)

    ssq = jnp.zeros((tm, 1), jnp.float32)
    for b in range(half_blocks):
        v = uv_ref[half_blocks + b]
        ssq += jnp.sum(v * v, axis=-1, keepdims=True)
    inv = lax.rsqrt(ssq / width + RMS_EPS)
    row = lax.broadcasted_iota(jnp.int32, (CHUNK, CHUNK), 0)
    col = lax.broadcasted_iota(jnp.int32, (CHUNK, CHUNK), 1)
    tril = (row >= col).astype(jnp.float32)
    for b in range(half_blocks):
        vn = ((uv_ref[half_blocks + b] * inv) * vg_ref[:, b * tc:(b + 1) * tc]).astype(jnp.bfloat16)
        for gl in range(groups_per_block):
            hg = b * groups_per_block + gl
            ws = (ws_ref[hg] * tril).astype(jnp.bfloat16)
            bias = bst_ref[:, hg:hg + 1]
            for c in range(tm // CHUNK):
                rs = slice(c * CHUNK, (c + 1) * CHUNK)
                gate = _bf16_dot(ws, vn[rs, gl * gw:(gl + 1) * gw]) + bias
                u = uv_ref[b, rs, gl * gw:(gl + 1) * gw]
                ug_ref[rs, hg * gw:(hg + 1) * gw] = (u * gate).astype(ug_ref.dtype)

    for k in range(o_ref.shape[1] // tc):
        cs = slice(k * tc, (k + 1) * tc)
        o_ref[:, cs] = x_ref[:, cs] + _bf16_dot(ug_ref[...], wout_ref[:, cs])


def _spatial_gate_block(x, g, w_in, v_norm, w_s, b_s, w_out):
    rows, d = x.shape
    width = w_out.shape[0]
    groups = w_s.shape[0]
    tm, tc = SG_ROWS, SG_TILE
    resident = dict(pipeline_mode=pl.Buffered(1))
    body = functools.partial(_spatial_gate_kernel, groups=groups)
    return pl.pallas_call(
        body,
        grid=(rows // tm,),
        in_specs=[
            pl.BlockSpec((tm, d), lambda i: (i, 0)),
            pl.BlockSpec((1, d), lambda i: (0, 0)),
            pl.BlockSpec((d, 2 * width), lambda i: (0, 0), **resident),
            pl.BlockSpec((1, width), lambda i: (0, 0)),
            pl.BlockSpec((groups, CHUNK, CHUNK), lambda i: (0, 0, 0)),
            pl.BlockSpec((CHUNK, groups), lambda i: (0, 0)),
            pl.BlockSpec((width, d), lambda i: (0, 0), **resident),
        ],
        out_specs=pl.BlockSpec((tm, d), lambda i: (i, 0)),
        out_shape=jax.ShapeDtypeStruct((rows, d), jnp.float32),
        scratch_shapes=[
            pltpu.VMEM((tm, d), jnp.bfloat16),
            pltpu.VMEM((2 * width // tc, tm, tc), jnp.float32),
            pltpu.VMEM((tm, width), jnp.bfloat16),
        ],
        compiler_params=pltpu.CompilerParams(
            dimension_semantics=("arbitrary",),
            vmem_limit_bytes=VMEM_LIMIT_BYTES),
        name="spatial_gate_mixer",
    )(x, g.reshape(1, d), w_in.astype(jnp.bfloat16), v_norm.reshape(1, width), w_s, b_s.T,
      w_out.astype(jnp.bfloat16))


def kernel(x, a_norm, a_in, a_conv, a_out, b_norm, b_in, b_vnorm, b_ws, b_bs, b_out,
           f_norm, f_up, f_conv_w, f_conv_b, f_down, final_norm):
    batch, seq, d = x.shape
    depth = f_norm.shape[0]
    assert seq % FFN_ROWS == 0 and seq % CONV_ROWS == 0 and seq % SG_ROWS == 0 and SG_ROWS % CHUNK == 0
    h = x.reshape(batch * seq, d)
    for layer in range(depth):
        m = layer // 2
        if layer % 2 == 0:
            h = _short_conv_block(h, a_norm[m], a_in[m], a_conv[m], a_out[m], seq=seq)
        else:
            h = _spatial_gate_block(h, b_norm[m], b_in[m], b_vnorm[m], b_ws[m], b_bs[m], b_out[m])
        h = _conv_ffn_block(h, f_norm[layer], f_up[layer], f_conv_w[layer], f_conv_b[layer],
                            f_down[layer], final_norm, seq=seq, final_norm=(layer == depth - 1))
    return h.reshape(batch, seq, d)
```

```python
import functools

import jax
import jax.numpy as jnp
from jax import lax
from jax.experimental import pallas as pl
from jax.experimental.pallas import tpu as pltpu

RMS_EPS = 1e-5
CHUNK = 128
SUBLANES = 8
LANES = 128
FFN_ROWS = 1024
FFN_TILE = 256
CONV_ROWS = 1024
CONV_TILE = 256
SG_ROWS = 512
SG_TILE = 1024
GATE_ROWS = 16
ROW_SPLIT = 4
GATE_CHAINS = 2
VMEM_LIMIT_BYTES = 58 * 1024 * 1024


def _rmsnorm(x, g):
    inv = lax.rsqrt(jnp.mean(x * x, axis=-1, keepdims=True) + RMS_EPS)
    return (x * inv) * g


def _bf16_dot(a, b):
    return jnp.dot(a, b, preferred_element_type=jnp.float32)


def _slab_dot(h_ref, w_refs):
    kb = w_refs[0].shape[0]
    acc = None
    for r, w_ref in enumerate(w_refs):
        part = _bf16_dot(h_ref[:, r * kb:(r + 1) * kb], w_ref[...])
        acc = part if acc is None else acc + part
    return acc


def _slab_specs(rows, cols, col_block):
    kb = rows // ROW_SPLIT
    return [pl.BlockSpec((kb, cols), functools.partial(lambda r, i, j: (r, col_block(j)), r))
            for r in range(ROW_SPLIT)]


def _panel_rows(rows, n_steps):
    return next(pr for pr in range(2 * SUBLANES, rows + 1, 2 * SUBLANES)
                if rows % pr == 0 and rows // pr <= n_steps)


def _cast_specs(weights, n_steps, steps_per_tile):
    specs, shapes, counts = [], [], []
    for w in weights:
        rows, cols = w.shape
        pr = _panel_rows(rows, n_steps)
        n = rows // pr
        specs.append(pl.BlockSpec((pr, cols), functools.partial(
            lambda n, i, j: (jnp.minimum(i * steps_per_tile + j, n - 1), 0), n)))
        shapes.append(jax.ShapeDtypeStruct(w.shape, jnp.bfloat16))
        counts.append(n)
    return specs, shapes, tuple(counts)


def _cast_panels(srcs, dsts, counts):
    step = pl.program_id(0) * pl.num_programs(1) + pl.program_id(1)
    for src, dst, n in zip(srcs, dsts, counts):
        @pl.when(step < n)
        def _(src=src, dst=dst):
            dst[...] = src[...].astype(dst.dtype)


def _take(refs, *counts):
    it = iter(refs)
    return [tuple(next(it) for _ in range(c)) for c in counts]


def _three_stage_step(j, nj, first, project, gate, contract, last):
    def step(parity, do_project, do_gate, do_contract):
        if do_gate:
            gate(1 - parity)
        if do_project:
            project(parity)
        if do_contract:
            contract(parity)

    @pl.when(j == 0)
    def _():
        first()
        step(0, True, False, False)

    @pl.when(j == 1)
    def _():
        step(1, True, True, False)

    for parity in range(2):
        @pl.when((j >= 2) & (j < nj) & (j % 2 == parity))
        def _(parity=parity):
            step(parity, True, True, True)

    @pl.when(j == nj)
    def _():
        step(nj % 2, False, True, True)

    @pl.when(j == nj + 1)
    def _():
        step((nj + 1) % 2, False, False, True)
        last()


def _stage_block(j, lag, nj):
    return jnp.clip(j - lag, 0, nj - 1)


def _swap_halo(raw, tail, cols, tm):
    raw[:SUBLANES, cols] = tail[...]
    tail[...] = raw[tm:, cols]


def _conv3_rows(raw, r0, rb, cols, w, token):
    win = _after(raw[r0 - SUBLANES:r0 + rb, cols], token)
    lo = SUBLANES
    return win[lo - 2:lo - 2 + rb] * w[0:1] + win[lo - 1:lo - 1 + rb] * w[1:2] + win[lo:] * w[2:3]


def _zero_bits():
    return jnp.zeros((SUBLANES, LANES), jnp.int32)


def _token(y, zero_bits):
    return (lax.bitcast_convert_type(y[:SUBLANES], jnp.int32) & zero_bits) == 0


def _after(x, token):
    if token is None:
        return x
    tiles = [jnp.where(token, x[k:k + SUBLANES], 0.0) for k in range(0, x.shape[0], SUBLANES)]
    return jnp.concatenate(tiles, axis=0)


def _short_conv_kernel(*refs, tiles_per_seq, tn, nj, cast_counts):
    nc = len(cast_counts)
    ((x_ref, g_ref, zero_ref), wb_refs, wc_refs, wx_refs, (cw_ref, wout_ref), cast_srcs, (o_ref,), cast_dsts,
     (h_ref, raw_even, raw_odd, act_even, act_odd, carry_ref)) = _take(
        refs, 3, ROW_SPLIT, ROW_SPLIT, ROW_SPLIT, 2, nc, 1, nc, 6)
    _cast_panels(cast_srcs, cast_dsts, cast_counts)
    i, j = pl.program_id(0), pl.program_id(1)
    raws, acts = (raw_even, raw_odd), (act_even, act_odd)
    tm = h_ref.shape[0]
    p_cols = slice(tn, 2 * tn)

    def first():
        x = x_ref[...]
        h_ref[...] = _rmsnorm(x, g_ref[...]).astype(h_ref.dtype)
        o_ref[...] = x

        @pl.when(i % tiles_per_seq == 0)
        def _():
            carry_ref[...] = jnp.zeros(carry_ref.shape, carry_ref.dtype)

    def project(parity):
        raw = raws[parity]
        raw[SUBLANES:, :tn] = _slab_dot(h_ref, wb_refs)
        raw[SUBLANES:, p_cols] = _slab_dot(h_ref, wc_refs) * _slab_dot(h_ref, wx_refs)

    def gate(parity):
        raw, act = raws[parity], acts[parity]
        _swap_halo(raw, carry_ref.at[j - 1], p_cols, tm)
        zero_bits, tokens = zero_ref[...], [None] * GATE_CHAINS
        for c in range(tn // LANES):
            cs = slice(c * LANES, (c + 1) * LANES)
            w = cw_ref[:, cs]
            for r in range(tm // GATE_ROWS):
                r0, token = SUBLANES + r * GATE_ROWS, tokens[r % GATE_CHAINS]
                conv = _conv3_rows(raw, r0, GATE_ROWS, slice(tn + c * LANES, tn + (c + 1) * LANES), w, token)
                y = _after(raw[r0:r0 + GATE_ROWS, cs], token) * conv
                tokens[r % GATE_CHAINS] = _token(y, zero_bits)
                act[r * GATE_ROWS:(r + 1) * GATE_ROWS, cs] = y.astype(act.dtype)

    def contract(parity):
        o_ref[...] += _bf16_dot(acts[parity][...], wout_ref[...])

    _three_stage_step(j, nj, first, project, gate, contract, lambda: None)


def _short_conv_block(x, g, w_in, w_conv, w_out, *, seq, cast=()):
    rows, d = x.shape
    tm, tn = CONV_ROWS, CONV_TILE
    nj = d // tn
    grid = (rows // tm, nj + 2)
    cast_specs, cast_shapes, cast_counts = _cast_specs(cast, grid[0] * grid[1], grid[1])
    body = functools.partial(_short_conv_kernel, tiles_per_seq=seq // tm, tn=tn, nj=nj,
                             cast_counts=cast_counts)
    return pl.pallas_call(
        body,
        grid=grid,
        in_specs=[
            pl.BlockSpec((tm, d), lambda i, j: (i, 0)),
            pl.BlockSpec((1, d), lambda i, j: (0, 0)),
            pl.BlockSpec((SUBLANES, LANES), lambda i, j: (0, 0)),
            *_slab_specs(d, tn, lambda j: _stage_block(j, 0, nj)),
            *_slab_specs(d, tn, lambda j: nj + _stage_block(j, 0, nj)),
            *_slab_specs(d, tn, lambda j: 2 * nj + _stage_block(j, 0, nj)),
            pl.BlockSpec((3, tn), lambda i, j: (0, _stage_block(j, 1, nj))),
            pl.BlockSpec((tn, d), lambda i, j: (_stage_block(j, 2, nj), 0)),
            *cast_specs,
        ],
        out_specs=[pl.BlockSpec((tm, d), lambda i, j: (i, 0)), *cast_specs],
        out_shape=[jax.ShapeDtypeStruct((rows, d), jnp.float32), *cast_shapes],
        scratch_shapes=[
            pltpu.VMEM((tm, d), jnp.bfloat16),
            pltpu.VMEM((tm + SUBLANES, 2 * tn), jnp.float32),
            pltpu.VMEM((tm + SUBLANES, 2 * tn), jnp.float32),
            pltpu.VMEM((tm, tn), jnp.bfloat16),
            pltpu.VMEM((tm, tn), jnp.bfloat16),
            pltpu.VMEM((nj, SUBLANES, tn), jnp.float32),
        ],
        compiler_params=pltpu.CompilerParams(
            dimension_semantics=("arbitrary", "arbitrary"),
            vmem_limit_bytes=VMEM_LIMIT_BYTES),
        name="short_conv_mixer",
    )(x, g.reshape(1, d), _zero_bits(), *([w_in] * (3 * ROW_SPLIT)), w_conv, w_out, *cast)


def _conv_ffn_kernel(*refs, tiles_per_seq, tf, nj, final_norm, cast_counts):
    nc = len(cast_counts)
    ((x_ref, g_ref, zero_ref), wg_refs, wa_refs, (cwg_ref, cwa_ref, cbg_ref, cba_ref, wdown_ref, fg_ref),
     cast_srcs, (o_ref,), cast_dsts, (h_ref, raw_even, raw_odd, act_even, act_odd, carry_ref)) = _take(
        refs, 3, ROW_SPLIT, ROW_SPLIT, 6, nc, 1, nc, 6)
    _cast_panels(cast_srcs, cast_dsts, cast_counts)
    i, j = pl.program_id(0), pl.program_id(1)
    raws, acts = (raw_even, raw_odd), (act_even, act_odd)
    tm = h_ref.shape[0]

    def first():
        x = x_ref[...]
        h_ref[...] = _rmsnorm(x, g_ref[...]).astype(h_ref.dtype)
        o_ref[...] = x

        @pl.when(i % tiles_per_seq == 0)
        def _():
            carry_ref[...] = jnp.zeros(carry_ref.shape, carry_ref.dtype)

    def project(parity):
        raw = raws[parity]
        raw[SUBLANES:, :tf] = _slab_dot(h_ref, wg_refs)
        raw[SUBLANES:, tf:] = _slab_dot(h_ref, wa_refs)

    def gate(parity):
        raw, act = raws[parity], acts[parity]
        _swap_halo(raw, carry_ref.at[j - 1], slice(None), tm)
        zero_bits, tokens = zero_ref[...], [None] * GATE_CHAINS
        for c in range(tf // LANES):
            cs = slice(c * LANES, (c + 1) * LANES)
            wts = [(cw_ref[:, cs], cb_ref[:, cs]) for cw_ref, cb_ref in ((cwg_ref, cbg_ref), (cwa_ref, cba_ref))]
            for r in range(tm // GATE_ROWS):
                r0, token = SUBLANES + r * GATE_ROWS, tokens[r % GATE_CHAINS]
                ga = [_conv3_rows(raw, r0, GATE_ROWS, slice(p * tf + c * LANES, p * tf + (c + 1) * LANES), w, token) + b
                      for p, (w, b) in enumerate(wts)]
                y = jax.nn.silu(ga[0]) * ga[1]
                tokens[r % GATE_CHAINS] = _token(y, zero_bits)
                act[r * GATE_ROWS:(r + 1) * GATE_ROWS, cs] = y.astype(act.dtype)

    def contract(parity):
        o_ref[...] += _bf16_dot(acts[parity][...], wdown_ref[...])

    def last():
        if final_norm:
            o_ref[...] = _rmsnorm(o_ref[...], fg_ref[...])

    _three_stage_step(j, nj, first, project, gate, contract, last)


def _conv_ffn_block(x, g, w_up, conv_w, conv_b, w_down, final_g, *, seq, final_norm, cast=()):
    rows, d = x.shape
    d_ff = w_down.shape[0]
    tm, tf = FFN_ROWS, FFN_TILE
    nj = d_ff // tf
    grid = (rows // tm, nj + 2)
    cb = conv_b.reshape(1, 2 * d_ff)
    cast_specs, cast_shapes, cast_counts = _cast_specs(cast, grid[0] * grid[1], grid[1])
    body = functools.partial(_conv_ffn_kernel, tiles_per_seq=seq // tm, tf=tf, nj=nj,
                             final_norm=final_norm, cast_counts=cast_counts)
    return pl.pallas_call(
        body,
        grid=grid,
        in_specs=[
            pl.BlockSpec((tm, d), lambda i, j: (i, 0)),
            pl.BlockSpec((1, d), lambda i, j: (0, 0)),
            pl.BlockSpec((SUBLANES, LANES), lambda i, j: (0, 0)),
            *_slab_specs(d, tf, lambda j: _stage_block(j, 0, nj)),
            *_slab_specs(d, tf, lambda j: nj + _stage_block(j, 0, nj)),
            pl.BlockSpec((3, tf), lambda i, j: (0, _stage_block(j, 1, nj))),
            pl.BlockSpec((3, tf), lambda i, j: (0, nj + _stage_block(j, 1, nj))),
            pl.BlockSpec((1, tf), lambda i, j: (0, _stage_block(j, 1, nj))),
            pl.BlockSpec((1, tf), lambda i, j: (0, nj + _stage_block(j, 1, nj))),
            pl.BlockSpec((tf, d), lambda i, j: (_stage_block(j, 2, nj), 0)),
            pl.BlockSpec((1, d), lambda i, j: (0, 0)),
            *cast_specs,
        ],
        out_specs=[pl.BlockSpec((tm, d), lambda i, j: (i, 0)), *cast_specs],
        out_shape=[jax.ShapeDtypeStruct((rows, d), jnp.float32), *cast_shapes],
        scratch_shapes=[
            pltpu.VMEM((tm, d), jnp.bfloat16),
            pltpu.VMEM((tm + SUBLANES, 2 * tf), jnp.float32),
            pltpu.VMEM((tm + SUBLANES, 2 * tf), jnp.float32),
            pltpu.VMEM((tm, tf), jnp.bfloat16),
            pltpu.VMEM((tm, tf), jnp.bfloat16),
            pltpu.VMEM((nj, SUBLANES, 2 * tf), jnp.float32),
        ],
        compiler_params=pltpu.CompilerParams(
            dimension_semantics=("arbitrary", "arbitrary"),
            vmem_limit_bytes=VMEM_LIMIT_BYTES),
        name="conv_ffn",
    )(x, g.reshape(1, d), _zero_bits(), *([w_up] * (2 * ROW_SPLIT)), conv_w, conv_w, cb, cb,
      w_down, final_g.reshape(1, d), *cast)


def _spatial_gate_kernel(x_ref, g_ref, win_ref, vg_ref, ws_ref, bst_ref, wout_ref, o_ref,
                         h_ref, uv_ref, ug_ref, *, groups):
    n_in, tm, tc = uv_ref.shape
    width = wout_ref.shape[0]
    half_blocks = n_in // 2
    gw = width // groups
    groups_per_block = tc // gw

    h_ref[...] = _rmsnorm(x_ref[...], g_ref[...]).astype(h_ref.dtype)

    def project(c):
        uv_ref[c] = _bf16_dot(h_ref[...], win_ref[:, c * tc:(c + 1) * tc])

    def gelu(c):
        uv_ref[c] = jax.nn.gelu(uv_ref[c])

    project(0)
    for c in range(1, n_in):
        gelu(c - 1)
        project(c)
    gelu(n_in - 1)

    ssq = jnp.zeros((tm, 1), jnp.float32)
    for b in range(half_blocks):
        v = uv_ref[half_blocks + b]
        ssq += jnp.sum(v * v, axis=-1, keepdims=True)
    inv = lax.rsqrt(ssq / width + RMS_EPS)
    row = lax.broadcasted_iota(jnp.int32, (CHUNK, CHUNK), 0)
    col = lax.broadcasted_iota(jnp.int32, (CHUNK, CHUNK), 1)
    tril = (row >= col).astype(jnp.float32)
    for b in range(half_blocks):
        vn = ((uv_ref[half_blocks + b] * inv) * vg_ref[:, b * tc:(b + 1) * tc]).astype(jnp.bfloat16)
        for gl in range(groups_per_block):
            hg = b * groups_per_block + gl
            ws = (ws_ref[hg] * tril).astype(jnp.bfloat16)
            bias = bst_ref[:, hg:hg + 1]
            for c in range(tm // CHUNK):
                rs = slice(c * CHUNK, (c + 1) * CHUNK)
                gate = _bf16_dot(ws, vn[rs, gl * gw:(gl + 1) * gw]) + bias
                u = uv_ref[b, rs, gl * gw:(gl + 1) * gw]
                ug_ref[rs, hg * gw:(hg + 1) * gw] = (u * gate).astype(ug_ref.dtype)

    for k in range(o_ref.shape[1] // tc):
        cs = slice(k * tc, (k + 1) * tc)
        o_ref[:, cs] = x_ref[:, cs] + _bf16_dot(ug_ref[...], wout_ref[:, cs])


def _spatial_gate_block(x, g, w_in, v_norm, w_s, b_s, w_out):
    rows, d = x.shape
    width = w_out.shape[0]
    groups = w_s.shape[0]
    tm, tc = SG_ROWS, SG_TILE
    resident = dict(pipeline_mode=pl.Buffered(1))
    body = functools.partial(_spatial_gate_kernel, groups=groups)
    return pl.pallas_call(
        body,
        grid=(rows // tm,),
        in_specs=[
            pl.BlockSpec((tm, d), lambda i: (i, 0)),
            pl.BlockSpec((1, d), lambda i: (0, 0)),
            pl.BlockSpec((d, 2 * width), lambda i: (0, 0), **resident),
            pl.BlockSpec((1, width), lambda i: (0, 0)),
            pl.BlockSpec((groups, CHUNK, CHUNK), lambda i: (0, 0, 0)),
            pl.BlockSpec((CHUNK, groups), lambda i: (0, 0)),
            pl.BlockSpec((width, d), lambda i: (0, 0), **resident),
        ],
        out_specs=pl.BlockSpec((tm, d), lambda i: (i, 0)),
        out_shape=jax.ShapeDtypeStruct((rows, d), jnp.float32),
        scratch_shapes=[
            pltpu.VMEM((tm, d), jnp.bfloat16),
            pltpu.VMEM((2 * width // tc, tm, tc), jnp.float32),
            pltpu.VMEM((tm, width), jnp.bfloat16),
        ],
        compiler_params=pltpu.CompilerParams(
            dimension_semantics=("arbitrary",),
            vmem_limit_bytes=VMEM_LIMIT_BYTES),
        name="spatial_gate_mixer",
    )(x, g.reshape(1, d), w_in, v_norm.reshape(1, width), w_s, b_s.T, w_out)


def kernel(x, a_norm, a_in, a_conv, a_out, b_norm, b_in, b_vnorm, b_ws, b_bs, b_out,
           f_norm, f_up, f_conv_w, f_conv_b, f_down, final_norm):
    batch, seq, d = x.shape
    depth = f_norm.shape[0]
    assert seq % FFN_ROWS == 0 and seq % CONV_ROWS == 0 and seq % SG_ROWS == 0 and SG_ROWS % CHUNK == 0

    def mixer_weights(layer):
        m = layer // 2
        return (a_in[m], a_out[m]) if layer % 2 == 0 else (b_in[m], b_out[m])

    h = x.reshape(batch * seq, d)
    w_mix = tuple(w.astype(jnp.bfloat16) for w in mixer_weights(0))
    w_ffn = ()
    for layer in range(depth):
        m, nxt = layer // 2, layer + 1
        to_cast = () if w_ffn else (f_up[layer], f_down[layer])
        if layer % 2 == 0:
            h, *done = _short_conv_block(h, a_norm[m], w_mix[0], a_conv[m], w_mix[1], seq=seq, cast=to_cast)
            w_ffn = w_ffn or tuple(done)
        else:
            assert not to_cast
            h = _spatial_gate_block(h, b_norm[m], w_mix[0], b_vnorm[m], b_ws[m], b_bs[m], w_mix[1])
        to_cast = (*mixer_weights(nxt), f_up[nxt], f_down[nxt]) if nxt < depth else ()
        h, *done = _conv_ffn_block(h, f_norm[layer], w_ffn[0], f_conv_w[layer], f_conv_b[layer], w_ffn[1],
                                   final_norm, seq=seq, final_norm=(nxt == depth), cast=to_cast)
        w_mix, w_ffn = tuple(done[:2]), tuple(done[2:])
    return h.reshape(batch, seq, d)
```

```python
import functools

import jax
import jax.numpy as jnp
from jax import lax
from jax.experimental import pallas as pl
from jax.experimental.pallas import tpu as pltpu

RMS_EPS = 1e-5
CHUNK = 128
SUBLANES = 8
LANES = 128
FFN_ROWS = 1024
FFN_TILE = 256
CONV_ROWS = 1024
CONV_TILE = 256
SG_ROWS = 512
SG_TILE = 1024
GATE_ROWS = 16
ROW_SPLIT = 4
GATE_CHAINS = 2
VMEM_LIMIT_BYTES = 58 * 1024 * 1024


def _rmsnorm(x, g):
    inv = lax.rsqrt(jnp.mean(x * x, axis=-1, keepdims=True) + RMS_EPS)
    return (x * inv) * g


def _bf16_dot(a, b):
    return jnp.dot(a, b, preferred_element_type=jnp.float32)


def _slab_dot(h_ref, w_refs):
    kb = w_refs[0].shape[0]
    acc = None
    for r, w_ref in enumerate(w_refs):
        part = _bf16_dot(h_ref[:, r * kb:(r + 1) * kb], w_ref[...])
        acc = part if acc is None else acc + part
    return acc


def _slab_specs(rows, cols, col_block):
    kb = rows // ROW_SPLIT
    return [pl.BlockSpec((kb, cols), functools.partial(lambda r, i, j: (r, col_block(j)), r))
            for r in range(ROW_SPLIT)]


def _panel_rows(rows, n_steps):
    return next(pr for pr in range(2 * SUBLANES, rows + 1, 2 * SUBLANES)
                if rows % pr == 0 and rows // pr <= n_steps)


def _cast_specs(weights, n_steps, steps_per_tile):
    in_specs, out_specs, shapes, counts = [], [], [], []
    for stack, layer in weights:
        _, rows, cols = stack.shape
        pr = _panel_rows(rows, n_steps)
        n = rows // pr

        def panel(i, j, n=n):
            return jnp.minimum(i * steps_per_tile + j, n - 1)

        in_specs.append(pl.BlockSpec((None, pr, cols), lambda i, j, layer=layer, panel=panel: (layer, panel(i, j), 0)))
        out_specs.append(pl.BlockSpec((pr, cols), lambda i, j, panel=panel: (panel(i, j), 0)))
        shapes.append(jax.ShapeDtypeStruct((rows, cols), jnp.bfloat16))
        counts.append(n)
    return in_specs, out_specs, shapes, tuple(counts)


def _cast_panels(srcs, dsts, counts):
    step = pl.program_id(0) * pl.num_programs(1) + pl.program_id(1)
    for src, dst, n in zip(srcs, dsts, counts):
        @pl.when(step < n)
        def _(src=src, dst=dst):
            dst[...] = src[...].astype(dst.dtype)


def _take(refs, *counts):
    it = iter(refs)
    return [tuple(next(it) for _ in range(c)) for c in counts]


def _three_stage_step(j, nj, first, project, gate, contract, last):
    def step(parity, do_project, do_gate, do_contract):
        if do_gate:
            gate(1 - parity)
        if do_project:
            project(parity)
        if do_contract:
            contract(parity)

    @pl.when(j == 0)
    def _():
        first()
        step(0, True, False, False)

    @pl.when(j == 1)
    def _():
        step(1, True, True, False)

    for parity in range(2):
        @pl.when((j >= 2) & (j < nj) & (j % 2 == parity))
        def _(parity=parity):
            step(parity, True, True, True)

    @pl.when(j == nj)
    def _():
        step(nj % 2, False, True, True)

    @pl.when(j == nj + 1)
    def _():
        step((nj + 1) % 2, False, False, True)
        last()


def _stage_block(j, lag, nj):
    return jnp.clip(j - lag, 0, nj - 1)


def _swap_halo(raw, tail, cols, tm):
    raw[:SUBLANES, cols] = tail[...]
    tail[...] = raw[tm:, cols]


def _conv3_rows(raw, r0, rb, cols, w, token):
    win = _after(raw[r0 - SUBLANES:r0 + rb, cols], token)
    lo = SUBLANES
    return win[lo - 2:lo - 2 + rb] * w[0:1] + win[lo - 1:lo - 1 + rb] * w[1:2] + win[lo:] * w[2:3]


def _zero_bits():
    return jnp.zeros((SUBLANES, LANES), jnp.int32)


def _token(y, zero_bits):
    bits = lax.bitcast_convert_type(y[:SUBLANES], jnp.int32) & zero_bits
    return lax.bitcast_convert_type(bits, jnp.float32)


def _after(x, token):
    if token is None:
        return x
    return x + jnp.concatenate([token] * (x.shape[0] // SUBLANES), axis=0)


def _short_conv_kernel(*refs, tiles_per_seq, tn, nj, cast_counts):
    nc = len(cast_counts)
    ((x_ref, g_ref, zero_ref), wb_refs, wc_refs, wx_refs, (cw_ref, wout_ref), cast_srcs, (o_ref,), cast_dsts,
     (h_ref, raw_even, raw_odd, act_even, act_odd, carry_ref)) = _take(
        refs, 3, ROW_SPLIT, ROW_SPLIT, ROW_SPLIT, 2, nc, 1, nc, 6)
    _cast_panels(cast_srcs, cast_dsts, cast_counts)
    i, j = pl.program_id(0), pl.program_id(1)
    raws, acts = (raw_even, raw_odd), (act_even, act_odd)
    tm = h_ref.shape[0]
    p_cols = slice(tn, 2 * tn)

    def first():
        x = x_ref[...]
        h_ref[...] = _rmsnorm(x, g_ref[...]).astype(h_ref.dtype)
        o_ref[...] = x

        @pl.when(i % tiles_per_seq == 0)
        def _():
            carry_ref[...] = jnp.zeros(carry_ref.shape, carry_ref.dtype)

    def project(parity):
        raw = raws[parity]
        raw[SUBLANES:, :tn] = _slab_dot(h_ref, wb_refs)
        raw[SUBLANES:, p_cols] = _slab_dot(h_ref, wc_refs) * _slab_dot(h_ref, wx_refs)

    def gate(parity):
        raw, act = raws[parity], acts[parity]
        _swap_halo(raw, carry_ref.at[j - 1], p_cols, tm)
        zero_bits, tokens = zero_ref[...], [None] * GATE_CHAINS
        for c in range(tn // LANES):
            cs = slice(c * LANES, (c + 1) * LANES)
            w = cw_ref[:, cs]
            for r in range(tm // GATE_ROWS):
                r0, token = SUBLANES + r * GATE_ROWS, tokens[r % GATE_CHAINS]
                conv = _conv3_rows(raw, r0, GATE_ROWS, slice(tn + c * LANES, tn + (c + 1) * LANES), w, token)
                y = _after(raw[r0:r0 + GATE_ROWS, cs], token) * conv
                tokens[r % GATE_CHAINS] = _token(y, zero_bits)
                act[r * GATE_ROWS:(r + 1) * GATE_ROWS, cs] = y.astype(act.dtype)

    def contract(parity):
        o_ref[...] += _bf16_dot(acts[parity][...], wout_ref[...])

    _three_stage_step(j, nj, first, project, gate, contract, lambda: None)


def _short_conv_block(x, g, w_in, w_conv, w_out, *, seq, cast=()):
    rows, d = x.shape
    tm, tn = CONV_ROWS, CONV_TILE
    nj = d // tn
    grid = (rows // tm, nj + 2)
    cast_in, cast_out, cast_shapes, cast_counts = _cast_specs(cast, grid[0] * grid[1], grid[1])
    body = functools.partial(_short_conv_kernel, tiles_per_seq=seq // tm, tn=tn, nj=nj,
                             cast_counts=cast_counts)
    return pl.pallas_call(
        body,
        grid=grid,
        in_specs=[
            pl.BlockSpec((tm, d), lambda i, j: (i, 0)),
            pl.BlockSpec((1, d), lambda i, j: (0, 0)),
            pl.BlockSpec((SUBLANES, LANES), lambda i, j: (0, 0)),
            *_slab_specs(d, tn, lambda j: _stage_block(j, 0, nj)),
            *_slab_specs(d, tn, lambda j: nj + _stage_block(j, 0, nj)),
            *_slab_specs(d, tn, lambda j: 2 * nj + _stage_block(j, 0, nj)),
            pl.BlockSpec((3, tn), lambda i, j: (0, _stage_block(j, 1, nj))),
            pl.BlockSpec((tn, d), lambda i, j: (_stage_block(j, 2, nj), 0)),
            *cast_in,
        ],
        out_specs=[pl.BlockSpec((tm, d), lambda i, j: (i, 0)), *cast_out],
        out_shape=[jax.ShapeDtypeStruct((rows, d), jnp.float32), *cast_shapes],
        scratch_shapes=[
            pltpu.VMEM((tm, d), jnp.bfloat16),
            pltpu.VMEM((tm + SUBLANES, 2 * tn), jnp.float32),
            pltpu.VMEM((tm + SUBLANES, 2 * tn), jnp.float32),
            pltpu.VMEM((tm, tn), jnp.bfloat16),
            pltpu.VMEM((tm, tn), jnp.bfloat16),
            pltpu.VMEM((nj, SUBLANES, tn), jnp.float32),
        ],
        compiler_params=pltpu.CompilerParams(
            dimension_semantics=("arbitrary", "arbitrary"),
            vmem_limit_bytes=VMEM_LIMIT_BYTES),
        name="short_conv_mixer",
    )(x, g.reshape(1, d), _zero_bits(), *([w_in] * (3 * ROW_SPLIT)), w_conv, w_out, *(stack for stack, _ in cast))


def _conv_ffn_kernel(*refs, tiles_per_seq, tf, nj, final_norm, cast_counts):
    nc = len(cast_counts)
    ((x_ref, g_ref, zero_ref), wg_refs, wa_refs, (cwg_ref, cwa_ref, cbg_ref, cba_ref, wdown_ref, fg_ref),
     cast_srcs, (o_ref,), cast_dsts, (h_ref, raw_even, raw_odd, act_even, act_odd, carry_ref)) = _take(
        refs, 3, ROW_SPLIT, ROW_SPLIT, 6, nc, 1, nc, 6)
    _cast_panels(cast_srcs, cast_dsts, cast_counts)
    i, j = pl.program_id(0), pl.program_id(1)
    raws, acts = (raw_even, raw_odd), (act_even, act_odd)
    tm = h_ref.shape[0]

    def first():
        x = x_ref[...]
        h_ref[...] = _rmsnorm(x, g_ref[...]).astype(h_ref.dtype)
        o_ref[...] = x

        @pl.when(i % tiles_per_seq == 0)
        def _():
            carry_ref[...] = jnp.zeros(carry_ref.shape, carry_ref.dtype)

    def project(parity):
        raw = raws[parity]
        raw[SUBLANES:, :tf] = _slab_dot(h_ref, wg_refs)
        raw[SUBLANES:, tf:] = _slab_dot(h_ref, wa_refs)

    def gate(parity):
        raw, act = raws[parity], acts[parity]
        _swap_halo(raw, carry_ref.at[j - 1], slice(None), tm)
        zero_bits, tokens = zero_ref[...], [None] * GATE_CHAINS
        for c in range(tf // LANES):
            cs = slice(c * LANES, (c + 1) * LANES)
            wts = [(cw_ref[:, cs], cb_ref[:, cs]) for cw_ref, cb_ref in ((cwg_ref, cbg_ref), (cwa_ref, cba_ref))]
            for r in range(tm // GATE_ROWS):
                r0, token = SUBLANES + r * GATE_ROWS, tokens[r % GATE_CHAINS]
                ga = [_conv3_rows(raw, r0, GATE_ROWS, slice(p * tf + c * LANES, p * tf + (c + 1) * LANES), w, token) + b
                      for p, (w, b) in enumerate(wts)]
                y = jax.nn.silu(ga[0]) * ga[1]
                tokens[r % GATE_CHAINS] = _token(y, zero_bits)
                act[r * GATE_ROWS:(r + 1) * GATE_ROWS, cs] = y.astype(act.dtype)

    def contract(parity):
        o_ref[...] += _bf16_dot(acts[parity][...], wdown_ref[...])

    def last():
        if final_norm:
            o_ref[...] = _rmsnorm(o_ref[...], fg_ref[...])

    _three_stage_step(j, nj, first, project, gate, contract, last)


def _conv_ffn_block(x, g, w_up, conv_w, conv_b, w_down, final_g, *, seq, final_norm, cast=()):
    rows, d = x.shape
    d_ff = w_down.shape[0]
    tm, tf = FFN_ROWS, FFN_TILE
    nj = d_ff // tf
    grid = (rows // tm, nj + 2)
    cb = conv_b.reshape(1, 2 * d_ff)
    cast_in, cast_out, cast_shapes, cast_counts = _cast_specs(cast, grid[0] * grid[1], grid[1])
    body = functools.partial(_conv_ffn_kernel, tiles_per_seq=seq // tm, tf=tf, nj=nj,
                             final_norm=final_norm, cast_counts=cast_counts)
    return pl.pallas_call(
        body,
        grid=grid,
        in_specs=[
            pl.BlockSpec((tm, d), lambda i, j: (i, 0)),
            pl.BlockSpec((1, d), lambda i, j: (0, 0)),
            pl.BlockSpec((SUBLANES, LANES), lambda i, j: (0, 0)),
            *_slab_specs(d, tf, lambda j: _stage_block(j, 0, nj)),
            *_slab_specs(d, tf, lambda j: nj + _stage_block(j, 0, nj)),
            pl.BlockSpec((3, tf), lambda i, j: (0, _stage_block(j, 1, nj))),
            pl.BlockSpec((3, tf), lambda i, j: (0, nj + _stage_block(j, 1, nj))),
            pl.BlockSpec((1, tf), lambda i, j: (0, _stage_block(j, 1, nj))),
            pl.BlockSpec((1, tf), lambda i, j: (0, nj + _stage_block(j, 1, nj))),
            pl.BlockSpec((tf, d), lambda i, j: (_stage_block(j, 2, nj), 0)),
            pl.BlockSpec((1, d), lambda i, j: (0, 0)),
            *cast_in,
        ],
        out_specs=[pl.BlockSpec((tm, d), lambda i, j: (i, 0)), *cast_out],
        out_shape=[jax.ShapeDtypeStruct((rows, d), jnp.float32), *cast_shapes],
        scratch_shapes=[
            pltpu.VMEM((tm, d), jnp.bfloat16),
            pltpu.VMEM((tm + SUBLANES, 2 * tf), jnp.float32),
            pltpu.VMEM((tm + SUBLANES, 2 * tf), jnp.float32),
            pltpu.VMEM((tm, tf), jnp.bfloat16),
            pltpu.VMEM((tm, tf), jnp.bfloat16),
            pltpu.VMEM((nj, SUBLANES, 2 * tf), jnp.float32),
        ],
        compiler_params=pltpu.CompilerParams(
            dimension_semantics=("arbitrary", "arbitrary"),
            vmem_limit_bytes=VMEM_LIMIT_BYTES),
        name="conv_ffn",
    )(x, g.reshape(1, d), _zero_bits(), *([w_up] * (2 * ROW_SPLIT)), conv_w, conv_w, cb, cb,
      w_down, final_g.reshape(1, d), *(stack for stack, _ in cast))


def _spatial_gate_kernel(x_ref, g_ref, win_ref, vg_ref, ws_ref, bst_ref, wout_ref, o_ref,
                         h_ref, uv_ref, ug_ref, *, groups):
    n_in, tm, tc = uv_ref.shape
    width = wout_ref.shape[0]
    half_blocks = n_in // 2
    gw = width // groups
    groups_per_block = tc // gw

    h_ref[...] = _rmsnorm(x_ref[...], g_ref[...]).astype(h_ref.dtype)

    def project(c):
        uv_ref[c] = _bf16_dot(h_ref[...], win_ref[:, c * tc:(c + 1) * tc])

    def gelu(c):
        uv_ref[c] = jax.nn.gelu(uv_ref[c])

    project(0)
    for c in range(1, n_in):
        gelu(c - 1)
        project(c)
    gelu(n_in - 1)

    ssq = jnp.zeros((tm, 1), jnp.float32)
    for b in range(half_blocks):
        v = uv_ref[half_blocks + b]
        ssq += jnp.sum(v * v, axis=-1, keepdims=True)
    inv = lax.rsqrt(ssq / width + RMS_EPS)
    row = lax.broadcasted_iota(jnp.int32, (CHUNK, CHUNK), 0)
    col = lax.broadcasted_iota(jnp.int32, (CHUNK, CHUNK), 1)
    tril = (row >= col).astype(jnp.float32)
    for b in range(half_blocks):
        vn = ((uv_ref[half_blocks + b] * inv) * vg_ref[:, b * tc:(b + 1) * tc]).astype(jnp.bfloat16)
        for gl in range(groups_per_block):
            hg = b * groups_per_block + gl
            ws = (ws_ref[hg] * tril).astype(jnp.bfloat16)
            bias = bst_ref[:, hg:hg + 1]
            for c in range(tm // CHUNK):
                rs = slice(c * CHUNK, (c + 1) * CHUNK)
                gate = _bf16_dot(ws, vn[rs, gl * gw:(gl + 1) * gw]) + bias
                u = uv_ref[b, rs, gl * gw:(gl + 1) * gw]
                ug_ref[rs, hg * gw:(hg + 1) * gw] = (u * gate).astype(ug_ref.dtype)

    for k in range(o_ref.shape[1] // tc):
        cs = slice(k * tc, (k + 1) * tc)
        o_ref[:, cs] = x_ref[:, cs] + _bf16_dot(ug_ref[...], wout_ref[:, cs])


def _spatial_gate_block(x, g, w_in, v_norm, w_s, b_s, w_out):
    rows, d = x.shape
    width = w_out.shape[0]
    groups = w_s.shape[0]
    tm, tc = SG_ROWS, SG_TILE
    resident = dict(pipeline_mode=pl.Buffered(1))
    body = functools.partial(_spatial_gate_kernel, groups=groups)
    return pl.pallas_call(
        body,
        grid=(rows // tm,),
        in_specs=[
            pl.BlockSpec((tm, d), lambda i: (i, 0)),
            pl.BlockSpec((1, d), lambda i: (0, 0)),
            pl.BlockSpec((d, 2 * width), lambda i: (0, 0), **resident),
            pl.BlockSpec((1, width), lambda i: (0, 0)),
            pl.BlockSpec((groups, CHUNK, CHUNK), lambda i: (0, 0, 0)),
            pl.BlockSpec((CHUNK, groups), lambda i: (0, 0)),
            pl.BlockSpec((width, d), lambda i: (0, 0), **resident),
        ],
        out_specs=pl.BlockSpec((tm, d), lambda i: (i, 0)),
        out_shape=jax.ShapeDtypeStruct((rows, d), jnp.float32),
        scratch_shapes=[
            pltpu.VMEM((tm, d), jnp.bfloat16),
            pltpu.VMEM((2 * width // tc, tm, tc), jnp.float32),
            pltpu.VMEM((tm, width), jnp.bfloat16),
        ],
        compiler_params=pltpu.CompilerParams(
            dimension_semantics=("arbitrary",),
            vmem_limit_bytes=VMEM_LIMIT_BYTES),
        name="spatial_gate_mixer",
    )(x, g.reshape(1, d), w_in, v_norm.reshape(1, width), w_s, b_s.T, w_out)


def kernel(x, a_norm, a_in, a_conv, a_out, b_norm, b_in, b_vnorm, b_ws, b_bs, b_out,
           f_norm, f_up, f_conv_w, f_conv_b, f_down, final_norm):
    batch, seq, d = x.shape
    depth = f_norm.shape[0]
    assert seq % FFN_ROWS == 0 and seq % CONV_ROWS == 0 and seq % SG_ROWS == 0 and SG_ROWS % CHUNK == 0

    def mixer_weights(layer):
        m = layer // 2
        return ((a_in, m), (a_out, m)) if layer % 2 == 0 else ((b_in, m), (b_out, m))

    h = x.reshape(batch * seq, d)
    w_mix = tuple(stack[k].astype(jnp.bfloat16) for stack, k in mixer_weights(0))
    w_ffn = ()
    for layer in range(depth):
        m, nxt = layer // 2, layer + 1
        to_cast = () if w_ffn else ((f_up, layer), (f_down, layer))
        if layer % 2 == 0:
            h, *done = _short_conv_block(h, a_norm[m], w_mix[0], a_conv[m], w_mix[1], seq=seq, cast=to_cast)
            w_ffn = w_ffn or tuple(done)
        else:
            assert not to_cast
            h = _spatial_gate_block(h, b_norm[m], w_mix[0], b_vnorm[m], b_ws[m], b_bs[m], w_mix[1])
        to_cast = (*mixer_weights(nxt), (f_up, nxt), (f_down, nxt)) if nxt < depth else ()
        h, *done = _conv_ffn_block(h, f_norm[layer], w_ffn[0], f_conv_w[layer], f_conv_b[layer], w_ffn[1],
                                   final_norm, seq=seq, final_norm=(nxt == depth), cast=to_cast)
        w_mix, w_ffn = tuple(done[:2]), tuple(done[2:])
    return h.reshape(batch, seq, d)
```

```python
import functools

import jax
import jax.numpy as jnp
from jax import lax
from jax.experimental import pallas as pl
from jax.experimental.pallas import tpu as pltpu

RMS_EPS = 1e-5
CHUNK = 128
SUBLANES = 8
LANES = 128
FFN_ROWS = 1024
FFN_TILE = 256
CONV_ROWS = 1024
CONV_TILE = 256
SG_ROWS = 512
SG_TILE = 1024
GATE_ROWS = 16
GATE_CHAINS = 2
VMEM_LIMIT_BYTES = 58 * 1024 * 1024


def _rmsnorm(x, g):
    inv = lax.rsqrt(jnp.mean(x * x, axis=-1, keepdims=True) + RMS_EPS)
    return (x * inv) * g


def _bf16_dot(a, b):
    return jnp.dot(a, b, preferred_element_type=jnp.float32)


def _panel_rows(rows, n_steps):
    return next(pr for pr in range(2 * SUBLANES, rows + 1, 2 * SUBLANES)
                if rows % pr == 0 and rows // pr <= n_steps)


def _cast_specs(weights, n_steps, steps_per_tile):
    in_specs, out_specs, shapes, counts = [], [], [], []
    for stack, layer in weights:
        _, rows, cols = stack.shape
        pr = _panel_rows(rows, n_steps)
        n = rows // pr

        def panel(i, j, n=n):
            return jnp.minimum(i * steps_per_tile + j, n - 1)

        in_specs.append(pl.BlockSpec((None, pr, cols), lambda i, j, layer=layer, panel=panel: (layer, panel(i, j), 0)))
        out_specs.append(pl.BlockSpec((pr, cols), lambda i, j, panel=panel: (panel(i, j), 0)))
        shapes.append(jax.ShapeDtypeStruct((rows, cols), jnp.bfloat16))
        counts.append(n)
    return in_specs, out_specs, shapes, tuple(counts)


def _cast_panels(srcs, dsts, counts):
    step = pl.program_id(0) * pl.num_programs(1) + pl.program_id(1)
    for src, dst, n in zip(srcs, dsts, counts):
        @pl.when(step < n)
        def _(src=src, dst=dst):
            dst[...] = src[...].astype(dst.dtype)


def _take(refs, *counts):
    it = iter(refs)
    return [tuple(next(it) for _ in range(c)) for c in counts]


def _three_stage_step(j, nj, first, project, gate, contract, last):
    def step(parity, do_project, do_gate, do_contract):
        if do_gate:
            gate(1 - parity)
        if do_project:
            project(parity)
        if do_contract:
            contract(parity)

    @pl.when(j == 0)
    def _():
        first()
        step(0, True, False, False)

    @pl.when(j == 1)
    def _():
        step(1, True, True, False)

    for parity in range(2):
        @pl.when((j >= 2) & (j < nj) & (j % 2 == parity))
        def _(parity=parity):
            step(parity, True, True, True)

    @pl.when(j == nj)
    def _():
        step(nj % 2, False, True, True)

    @pl.when(j == nj + 1)
    def _():
        step((nj + 1) % 2, False, False, True)
        last()


def _stage_block(j, lag, nj):
    return jnp.clip(j - lag, 0, nj - 1)


def _swap_halo(raw, tail, cols, tm):
    raw[:SUBLANES, cols] = tail[...]
    tail[...] = raw[tm:, cols]


def _conv3_rows(raw, r0, rb, cols, w, token):
    win = _after(raw[r0 - SUBLANES:r0 + rb, cols], token)
    lo = SUBLANES
    return win[lo - 2:lo - 2 + rb] * w[0:1] + win[lo - 1:lo - 1 + rb] * w[1:2] + win[lo:] * w[2:3]


def _zero_bits():
    return jnp.zeros((SUBLANES, LANES), jnp.int32)


def _token(y, zero_bits):
    bits = lax.bitcast_convert_type(y[:SUBLANES], jnp.int32) & zero_bits
    return lax.bitcast_convert_type(bits, jnp.float32)


def _after(x, token):
    if token is None:
        return x
    return x + jnp.concatenate([token] * (x.shape[0] // SUBLANES), axis=0)


def _short_conv_kernel(*refs, tiles_per_seq, tn, nj, cast_counts):
    nc = len(cast_counts)
    ((x_ref, g_ref, zero_ref, wb_ref, wc_ref, wx_ref, cw_ref, wout_ref), cast_srcs, (o_ref,), cast_dsts,
     (h_ref, raw_even, raw_odd, act_even, act_odd, carry_ref)) = _take(refs, 8, nc, 1, nc, 6)
    _cast_panels(cast_srcs, cast_dsts, cast_counts)
    i, j = pl.program_id(0), pl.program_id(1)
    raws, acts = (raw_even, raw_odd), (act_even, act_odd)
    tm = h_ref.shape[0]
    p_cols = slice(tn, 2 * tn)

    def first():
        x = x_ref[...]
        h_ref[...] = _rmsnorm(x, g_ref[...]).astype(h_ref.dtype)
        o_ref[...] = x

        @pl.when(i % tiles_per_seq == 0)
        def _():
            carry_ref[...] = jnp.zeros(carry_ref.shape, carry_ref.dtype)

    def project(parity):
        raw, h = raws[parity], h_ref[...]
        raw[SUBLANES:, :tn] = _bf16_dot(h, wb_ref[...])
        raw[SUBLANES:, p_cols] = _bf16_dot(h, wc_ref[...]) * _bf16_dot(h, wx_ref[...])

    def gate(parity):
        raw, act = raws[parity], acts[parity]
        _swap_halo(raw, carry_ref.at[j - 1], p_cols, tm)
        zero_bits, tokens = zero_ref[...], [None] * GATE_CHAINS
        for c in range(tn // LANES):
            cs = slice(c * LANES, (c + 1) * LANES)
            w = cw_ref[:, cs]
            for r in range(tm // GATE_ROWS):
                r0, token = SUBLANES + r * GATE_ROWS, tokens[r % GATE_CHAINS]
                conv = _conv3_rows(raw, r0, GATE_ROWS, slice(tn + c * LANES, tn + (c + 1) * LANES), w, token)
                y = _after(raw[r0:r0 + GATE_ROWS, cs], token) * conv
                tokens[r % GATE_CHAINS] = _token(y, zero_bits)
                act[r * GATE_ROWS:(r + 1) * GATE_ROWS, cs] = y.astype(act.dtype)

    def contract(parity):
        o_ref[...] += _bf16_dot(acts[parity][...], wout_ref[...])

    _three_stage_step(j, nj, first, project, gate, contract, lambda: None)


def _short_conv_block(x, g, w_in, w_conv, w_out, *, seq, cast=()):
    rows, d = x.shape
    tm, tn = CONV_ROWS, CONV_TILE
    nj = d // tn
    grid = (rows // tm, nj + 2)
    cast_in, cast_out, cast_shapes, cast_counts = _cast_specs(cast, grid[0] * grid[1], grid[1])
    body = functools.partial(_short_conv_kernel, tiles_per_seq=seq // tm, tn=tn, nj=nj,
                             cast_counts=cast_counts)
    return pl.pallas_call(
        body,
        grid=grid,
        in_specs=[
            pl.BlockSpec((tm, d), lambda i, j: (i, 0)),
            pl.BlockSpec((1, d), lambda i, j: (0, 0)),
            pl.BlockSpec((SUBLANES, LANES), lambda i, j: (0, 0)),
            pl.BlockSpec((d, tn), lambda i, j: (0, _stage_block(j, 0, nj))),
            pl.BlockSpec((d, tn), lambda i, j: (0, nj + _stage_block(j, 0, nj))),
            pl.BlockSpec((d, tn), lambda i, j: (0, 2 * nj + _stage_block(j, 0, nj))),
            pl.BlockSpec((3, tn), lambda i, j: (0, _stage_block(j, 1, nj))),
            pl.BlockSpec((tn, d), lambda i, j: (_stage_block(j, 2, nj), 0)),
            *cast_in,
        ],
        out_specs=[pl.BlockSpec((tm, d), lambda i, j: (i, 0)), *cast_out],
        out_shape=[jax.ShapeDtypeStruct((rows, d), jnp.float32), *cast_shapes],
        scratch_shapes=[
            pltpu.VMEM((tm, d), jnp.bfloat16),
            pltpu.VMEM((tm + SUBLANES, 2 * tn), jnp.float32),
            pltpu.VMEM((tm + SUBLANES, 2 * tn), jnp.float32),
            pltpu.VMEM((tm, tn), jnp.bfloat16),
            pltpu.VMEM((tm, tn), jnp.bfloat16),
            pltpu.VMEM((nj, SUBLANES, tn), jnp.float32),
        ],
        compiler_params=pltpu.CompilerParams(
            dimension_semantics=("arbitrary", "arbitrary"),
            vmem_limit_bytes=VMEM_LIMIT_BYTES),
        name="short_conv_mixer",
    )(x, g.reshape(1, d), _zero_bits(), w_in, w_in, w_in, w_conv, w_out, *(stack for stack, _ in cast))


def _conv_ffn_kernel(*refs, tiles_per_seq, tf, nj, final_norm, cast_counts):
    nc = len(cast_counts)
    ((x_ref, g_ref, zero_ref, wg_ref, wa_ref, cwg_ref, cwa_ref, cbg_ref, cba_ref, wdown_ref, fg_ref),
     cast_srcs, (o_ref,), cast_dsts, (h_ref, raw_even, raw_odd, act_even, act_odd, carry_ref)) = _take(
        refs, 11, nc, 1, nc, 6)
    _cast_panels(cast_srcs, cast_dsts, cast_counts)
    i, j = pl.program_id(0), pl.program_id(1)
    raws, acts = (raw_even, raw_odd), (act_even, act_odd)
    tm = h_ref.shape[0]

    def first():
        x = x_ref[...]
        h_ref[...] = _rmsnorm(x, g_ref[...]).astype(h_ref.dtype)
        o_ref[...] = x

        @pl.when(i % tiles_per_seq == 0)
        def _():
            carry_ref[...] = jnp.zeros(carry_ref.shape, carry_ref.dtype)

    def project(parity):
        raw, h = raws[parity], h_ref[...]
        raw[SUBLANES:, :tf] = _bf16_dot(h, wg_ref[...])
        raw[SUBLANES:, tf:] = _bf16_dot(h, wa_ref[...])

    def gate(parity):
        raw, act = raws[parity], acts[parity]
        _swap_halo(raw, carry_ref.at[j - 1], slice(None), tm)
        zero_bits, tokens = zero_ref[...], [None] * GATE_CHAINS
        for c in range(tf // LANES):
            cs = slice(c * LANES, (c + 1) * LANES)
            wts = [(cw_ref[:, cs], cb_ref[:, cs]) for cw_ref, cb_ref in ((cwg_ref, cbg_ref), (cwa_ref, cba_ref))]
            for r in range(tm // GATE_ROWS):
                r0, token = SUBLANES + r * GATE_ROWS, tokens[r % GATE_CHAINS]
                ga = [_conv3_rows(raw, r0, GATE_ROWS, slice(p * tf + c * LANES, p * tf + (c + 1) * LANES), w, token) + b
                      for p, (w, b) in enumerate(wts)]
                y = jax.nn.silu(ga[0]) * ga[1]
                tokens[r % GATE_CHAINS] = _token(y, zero_bits)
                act[r * GATE_ROWS:(r + 1) * GATE_ROWS, cs] = y.astype(act.dtype)

    def contract(parity):
        o_ref[...] += _bf16_dot(acts[parity][...], wdown_ref[...])

    def last():
        if final_norm:
            o_ref[...] = _rmsnorm(o_ref[...], fg_ref[...])

    _three_stage_step(j, nj, first, project, gate, contract, last)


def _conv_ffn_block(x, g, w_up, conv_w, conv_b, w_down, final_g, *, seq, final_norm, cast=()):
    rows, d = x.shape
    d_ff = w_down.shape[0]
    tm, tf = FFN_ROWS, FFN_TILE
    nj = d_ff // tf
    grid = (rows // tm, nj + 2)
    cb = conv_b.reshape(1, 2 * d_ff)
    cast_in, cast_out, cast_shapes, cast_counts = _cast_specs(cast, grid[0] * grid[1], grid[1])
    body = functools.partial(_conv_ffn_kernel, tiles_per_seq=seq // tm, tf=tf, nj=nj,
                             final_norm=final_norm, cast_counts=cast_counts)
    return pl.pallas_call(
        body,
        grid=grid,
        in_specs=[
            pl.BlockSpec((tm, d), lambda i, j: (i, 0)),
            pl.BlockSpec((1, d), lambda i, j: (0, 0)),
            pl.BlockSpec((SUBLANES, LANES), lambda i, j: (0, 0)),
            pl.BlockSpec((d, tf), lambda i, j: (0, _stage_block(j, 0, nj))),
            pl.BlockSpec((d, tf), lambda i, j: (0, nj + _stage_block(j, 0, nj))),
            pl.BlockSpec((3, tf), lambda i, j: (0, _stage_block(j, 1, nj))),
            pl.BlockSpec((3, tf), lambda i, j: (0, nj + _stage_block(j, 1, nj))),
            pl.BlockSpec((1, tf), lambda i, j: (0, _stage_block(j, 1, nj))),
            pl.BlockSpec((1, tf), lambda i, j: (0, nj + _stage_block(j, 1, nj))),
            pl.BlockSpec((tf, d), lambda i, j: (_stage_block(j, 2, nj), 0)),
            pl.BlockSpec((1, d), lambda i, j: (0, 0)),
            *cast_in,
        ],
        out_specs=[pl.BlockSpec((tm, d), lambda i, j: (i, 0)), *cast_out],
        out_shape=[jax.ShapeDtypeStruct((rows, d), jnp.float32), *cast_shapes],
        scratch_shapes=[
            pltpu.VMEM((tm, d), jnp.bfloat16),
            pltpu.VMEM((tm + SUBLANES, 2 * tf), jnp.float32),
            pltpu.VMEM((tm + SUBLANES, 2 * tf), jnp.float32),
            pltpu.VMEM((tm, tf), jnp.bfloat16),
            pltpu.VMEM((tm, tf), jnp.bfloat16),
            pltpu.VMEM((nj, SUBLANES, 2 * tf), jnp.float32),
        ],
        compiler_params=pltpu.CompilerParams(
            dimension_semantics=("arbitrary", "arbitrary"),
            vmem_limit_bytes=VMEM_LIMIT_BYTES),
        name="conv_ffn",
    )(x, g.reshape(1, d), _zero_bits(), w_up, w_up, conv_w, conv_w, cb, cb,
      w_down, final_g.reshape(1, d), *(stack for stack, _ in cast))


def _spatial_gate_kernel(x_ref, g_ref, win_ref, vg_ref, ws_ref, bst_ref, wout_ref, o_ref,
                         h_ref, uv_ref, ug_ref, *, groups):
    n_in, tm, tc = uv_ref.shape
    width = wout_ref.shape[0]
    half_blocks = n_in // 2
    gw = width // groups
    groups_per_block = tc // gw

    h_ref[...] = _rmsnorm(x_ref[...], g_ref[...]).astype(h_ref.dtype)

    def project(c):
        uv_ref[c] = _bf16_dot(h_ref[...], win_ref[:, c * tc:(c + 1) * tc])

    def gelu(c):
        uv_ref[c] = jax.nn.gelu(uv_ref[c])

    project(0)
    for c in range(1, n_in):
        gelu(c - 1)
        project(c)
    gelu(n_in - 1)

    ssq = jnp.zeros((tm, 1), jnp.float32)
    for b in range(half_blocks):
        v = uv_ref[half_blocks + b]
        ssq += jnp.sum(v * v, axis=-1, keepdims=True)
    inv = lax.rsqrt(ssq / width + RMS_EPS)
    row = lax.broadcasted_iota(jnp.int32, (CHUNK, CHUNK), 0)
    col = lax.broadcasted_iota(jnp.int32, (CHUNK, CHUNK), 1)
    tril = (row >= col).astype(jnp.float32)
    for b in range(half_blocks):
        vn = ((uv_ref[half_blocks + b] * inv) * vg_ref[:, b * tc:(b + 1) * tc]).astype(jnp.bfloat16)
        for gl in range(groups_per_block):
            hg = b * groups_per_block + gl
            ws = (ws_ref[hg] * tril).astype(jnp.bfloat16)
            bias = bst_ref[:, hg:hg + 1]
            for c in range(tm // CHUNK):
                rs = slice(c * CHUNK, (c + 1) * CHUNK)
                gate = _bf16_dot(ws, vn[rs, gl * gw:(gl + 1) * gw]) + bias
                u = uv_ref[b, rs, gl * gw:(gl + 1) * gw]
                ug_ref[rs, hg * gw:(hg + 1) * gw] = (u * gate).astype(ug_ref.dtype)

    for k in range(o_ref.shape[1] // tc):
        cs = slice(k * tc, (k + 1) * tc)
        o_ref[:, cs] = x_ref[:, cs] + _bf16_dot(ug_ref[...], wout_ref[:, cs])


def _spatial_gate_block(x, g, w_in, v_norm, w_s, b_s, w_out):
    rows, d = x.shape
    width = w_out.shape[0]
    groups = w_s.shape[0]
    tm, tc = SG_ROWS, SG_TILE
    resident = dict(pipeline_mode=pl.Buffered(1))
    body = functools.partial(_spatial_gate_kernel, groups=groups)
    return pl.pallas_call(
        body,
        grid=(rows // tm,),
        in_specs=[
            pl.BlockSpec((tm, d), lambda i: (i, 0)),
            pl.BlockSpec((1, d), lambda i: (0, 0)),
            pl.BlockSpec((d, 2 * width), lambda i: (0, 0), **resident),
            pl.BlockSpec((1, width), lambda i: (0, 0)),
            pl.BlockSpec((groups, CHUNK, CHUNK), lambda i: (0, 0, 0)),
            pl.BlockSpec((CHUNK, groups), lambda i: (0, 0)),
            pl.BlockSpec((width, d), lambda i: (0, 0), **resident),
        ],
        out_specs=pl.BlockSpec((tm, d), lambda i: (i, 0)),
        out_shape=jax.ShapeDtypeStruct((rows, d), jnp.float32),
        scratch_shapes=[
            pltpu.VMEM((tm, d), jnp.bfloat16),
            pltpu.VMEM((2 * width // tc, tm, tc), jnp.float32),
            pltpu.VMEM((tm, width), jnp.bfloat16),
        ],
        compiler_params=pltpu.CompilerParams(
            dimension_semantics=("arbitrary",),
            vmem_limit_bytes=VMEM_LIMIT_BYTES),
        name="spatial_gate_mixer",
    )(x, g.reshape(1, d), w_in, v_norm.reshape(1, width), w_s, b_s.T, w_out)


def kernel(x, a_norm, a_in, a_conv, a_out, b_norm, b_in, b_vnorm, b_ws, b_bs, b_out,
           f_norm, f_up, f_conv_w, f_conv_b, f_down, final_norm):
    batch, seq, d = x.shape
    depth = f_norm.shape[0]
    assert seq % FFN_ROWS == 0 and seq % CONV_ROWS == 0 and seq % SG_ROWS == 0 and SG_ROWS % CHUNK == 0

    def mixer_weights(layer):
        m = layer // 2
        return ((a_in, m), (a_out, m)) if layer % 2 == 0 else ((b_in, m), (b_out, m))

    h = x.reshape(batch * seq, d)
    w_mix = tuple(stack[k].astype(jnp.bfloat16) for stack, k in mixer_weights(0))
    w_ffn = ()
    for layer in range(depth):
        m, nxt = layer // 2, layer + 1
        to_cast = () if w_ffn else ((f_up, layer), (f_down, layer))
        if layer % 2 == 0:
            h, *done = _short_conv_block(h, a_norm[m], w_mix[0], a_conv[m], w_mix[1], seq=seq, cast=to_cast)
            w_ffn = w_ffn or tuple(done)
        else:
            assert not to_cast
            h = _spatial_gate_block(h, b_norm[m], w_mix[0], b_vnorm[m], b_ws[m], b_bs[m], w_mix[1])
        to_cast = (*mixer_weights(nxt), (f_up, nxt), (f_down, nxt)) if nxt < depth else ()
        h, *done = _conv_ffn_block(h, f_norm[layer], w_ffn[0], f_conv_w[layer], f_conv_b[layer], w_ffn[1],
                                   final_norm, seq=seq, final_norm=(nxt == depth), cast=to_cast)
        w_mix, w_ffn = tuple(done[:2]), tuple(done[2:])
    return h.reshape(batch, seq, d)
```

```python
import functools

import jax
import jax.numpy as jnp
from jax import lax
from jax.experimental import pallas as pl
from jax.experimental.pallas import tpu as pltpu

RMS_EPS = 1e-5
CHUNK = 128
SUBLANES = 8
LANES = 128
FFN_ROWS = 1024
FFN_TILE = 256
CONV_ROWS = 1024
CONV_TILE = 256
SG_ROWS = 512
SG_TILE = 2048
GATE_ROWS = 16
GATE_CHAINS = 2
VMEM_LIMIT_BYTES = 58 * 1024 * 1024


def _rmsnorm(x, g):
    inv = lax.rsqrt(jnp.mean(x * x, axis=-1, keepdims=True) + RMS_EPS)
    return (x * inv) * g


def _bf16_dot(a, b):
    return jnp.dot(a, b, preferred_element_type=jnp.float32)


def _panel_rows(rows, n_steps):
    return next(pr for pr in range(2 * SUBLANES, rows + 1, 2 * SUBLANES)
                if rows % pr == 0 and rows // pr <= n_steps)


def _cast_specs(weights, n_steps, steps_per_tile):
    in_specs, out_specs, shapes, counts = [], [], [], []
    for stack, layer in weights:
        _, rows, cols = stack.shape
        pr = _panel_rows(rows, n_steps)
        n = rows // pr

        def panel(i, j, n=n):
            return jnp.minimum(i * steps_per_tile + j, n - 1)

        in_specs.append(pl.BlockSpec((None, pr, cols), lambda i, j, layer=layer, panel=panel: (layer, panel(i, j), 0)))
        out_specs.append(pl.BlockSpec((pr, cols), lambda i, j, panel=panel: (panel(i, j), 0)))
        shapes.append(jax.ShapeDtypeStruct((rows, cols), jnp.bfloat16))
        counts.append(n)
    return in_specs, out_specs, shapes, tuple(counts)


def _cast_panels(srcs, dsts, counts):
    step = pl.program_id(0) * pl.num_programs(1) + pl.program_id(1)
    for src, dst, n in zip(srcs, dsts, counts):
        @pl.when(step < n)
        def _(src=src, dst=dst):
            dst[...] = src[...].astype(dst.dtype)


def _take(refs, *counts):
    it = iter(refs)
    return [tuple(next(it) for _ in range(c)) for c in counts]


def _three_stage_step(j, nj, first, project, gate, contract, last):
    def step(parity, do_project, do_gate, do_contract):
        if do_gate:
            gate(1 - parity)
        if do_project:
            project(parity)
        if do_contract:
            contract(parity)

    @pl.when(j == 0)
    def _():
        first()
        step(0, True, False, False)

    @pl.when(j == 1)
    def _():
        step(1, True, True, False)

    for parity in range(2):
        @pl.when((j >= 2) & (j < nj) & (j % 2 == parity))
        def _(parity=parity):
            step(parity, True, True, True)

    @pl.when(j == nj)
    def _():
        step(nj % 2, False, True, True)

    @pl.when(j == nj + 1)
    def _():
        step((nj + 1) % 2, False, False, True)
        last()


def _step0_rows(n_tiles):
    return lambda i, j: (jnp.minimum(i + jnp.minimum(j, 1), n_tiles - 1), 0)


def _stage_block(j, lag, nj):
    return jnp.clip(j - lag, 0, nj - 1)


def _swap_halo(raw, tail, cols, tm):
    raw[:SUBLANES, cols] = tail[...]
    tail[...] = raw[tm:, cols]


def _conv3_rows(raw, r0, rb, cols, w, token):
    win = _after(raw[r0 - SUBLANES:r0 + rb, cols], token)
    lo = SUBLANES
    return win[lo - 2:lo - 2 + rb] * w[0:1] + win[lo - 1:lo - 1 + rb] * w[1:2] + win[lo:] * w[2:3]


def _zero_bits():
    return jnp.zeros((SUBLANES, LANES), jnp.int32)


def _token(y, zero_bits):
    bits = lax.bitcast_convert_type(y[:SUBLANES], jnp.int32) & zero_bits
    return lax.bitcast_convert_type(bits, jnp.float32)


def _after(x, token):
    if token is None:
        return x
    return x + jnp.concatenate([token] * (x.shape[0] // SUBLANES), axis=0)


def _short_conv_kernel(*refs, tiles_per_seq, tn, nj, cast_counts):
    nc = len(cast_counts)
    ((x_ref, g_ref, zero_ref, wb_ref, wc_ref, wx_ref, cw_ref, wout_ref), cast_srcs, (o_ref,), cast_dsts,
     (h_ref, raw_even, raw_odd, act_even, act_odd, carry_ref)) = _take(refs, 8, nc, 1, nc, 6)
    _cast_panels(cast_srcs, cast_dsts, cast_counts)
    i, j = pl.program_id(0), pl.program_id(1)
    raws, acts = (raw_even, raw_odd), (act_even, act_odd)
    tm = h_ref.shape[0]
    p_cols = slice(tn, 2 * tn)

    def first():
        x = x_ref[...]
        h_ref[...] = _rmsnorm(x, g_ref[...]).astype(h_ref.dtype)
        o_ref[...] = x

        @pl.when(i % tiles_per_seq == 0)
        def _():
            carry_ref[...] = jnp.zeros(carry_ref.shape, carry_ref.dtype)

    def project(parity):
        raw, h = raws[parity], h_ref[...]
        raw[SUBLANES:, :tn] = _bf16_dot(h, wb_ref[...])
        raw[SUBLANES:, p_cols] = _bf16_dot(h, wc_ref[...]) * _bf16_dot(h, wx_ref[...])

    def gate(parity):
        raw, act = raws[parity], acts[parity]
        _swap_halo(raw, carry_ref.at[j - 1], p_cols, tm)
        zero_bits, tokens = zero_ref[...], [None] * GATE_CHAINS
        for c in range(tn // LANES):
            cs = slice(c * LANES, (c + 1) * LANES)
            w = cw_ref[:, cs]
            for r in range(tm // GATE_ROWS):
                r0, token = SUBLANES + r * GATE_ROWS, tokens[r % GATE_CHAINS]
                conv = _conv3_rows(raw, r0, GATE_ROWS, slice(tn + c * LANES, tn + (c + 1) * LANES), w, token)
                y = _after(raw[r0:r0 + GATE_ROWS, cs], token) * conv
                tokens[r % GATE_CHAINS] = _token(y, zero_bits)
                act[r * GATE_ROWS:(r + 1) * GATE_ROWS, cs] = y.astype(act.dtype)

    def contract(parity):
        o_ref[...] += _bf16_dot(acts[parity][...], wout_ref[...])

    _three_stage_step(j, nj, first, project, gate, contract, lambda: None)


def _short_conv_block(x, g, w_in, w_conv, w_out, *, seq, cast=()):
    rows, d = x.shape
    tm, tn = CONV_ROWS, CONV_TILE
    nj = d // tn
    grid = (rows // tm, nj + 2)
    cast_in, cast_out, cast_shapes, cast_counts = _cast_specs(cast, grid[0] * grid[1], grid[1])
    body = functools.partial(_short_conv_kernel, tiles_per_seq=seq // tm, tn=tn, nj=nj,
                             cast_counts=cast_counts)
    return pl.pallas_call(
        body,
        grid=grid,
        in_specs=[
            pl.BlockSpec((tm, d), _step0_rows(grid[0])),
            pl.BlockSpec((1, d), lambda i, j: (0, 0)),
            pl.BlockSpec((SUBLANES, LANES), lambda i, j: (0, 0)),
            pl.BlockSpec((d, tn), lambda i, j: (0, _stage_block(j, 0, nj))),
            pl.BlockSpec((d, tn), lambda i, j: (0, nj + _stage_block(j, 0, nj))),
            pl.BlockSpec((d, tn), lambda i, j: (0, 2 * nj + _stage_block(j, 0, nj))),
            pl.BlockSpec((3, tn), lambda i, j: (0, _stage_block(j, 1, nj))),
            pl.BlockSpec((tn, d), lambda i, j: (_stage_block(j, 2, nj), 0)),
            *cast_in,
        ],
        out_specs=[pl.BlockSpec((tm, d), lambda i, j: (i, 0)), *cast_out],
        out_shape=[jax.ShapeDtypeStruct((rows, d), jnp.float32), *cast_shapes],
        scratch_shapes=[
            pltpu.VMEM((tm, d), jnp.bfloat16),
            pltpu.VMEM((tm + SUBLANES, 2 * tn), jnp.float32),
            pltpu.VMEM((tm + SUBLANES, 2 * tn), jnp.float32),
            pltpu.VMEM((tm, tn), jnp.bfloat16),
            pltpu.VMEM((tm, tn), jnp.bfloat16),
            pltpu.VMEM((nj, SUBLANES, tn), jnp.float32),
        ],
        compiler_params=pltpu.CompilerParams(
            dimension_semantics=("arbitrary", "arbitrary"),
            vmem_limit_bytes=VMEM_LIMIT_BYTES),
        name="short_conv_mixer",
    )(x, g.reshape(1, d), _zero_bits(), w_in, w_in, w_in, w_conv, w_out, *(stack for stack, _ in cast))


def _conv_ffn_kernel(*refs, tiles_per_seq, tf, nj, final_norm, cast_counts):
    nc = len(cast_counts)
    ((x_ref, g_ref, zero_ref, wg_ref, wa_ref, cwg_ref, cwa_ref, cbg_ref, cba_ref, wdown_ref, fg_ref),
     cast_srcs, (o_ref,), cast_dsts, (h_ref, raw_even, raw_odd, act_even, act_odd, carry_ref)) = _take(
        refs, 11, nc, 1, nc, 6)
    _cast_panels(cast_srcs, cast_dsts, cast_counts)
    i, j = pl.program_id(0), pl.program_id(1)
    raws, acts = (raw_even, raw_odd), (act_even, act_odd)
    tm = h_ref.shape[0]

    def first():
        x = x_ref[...]
        h_ref[...] = _rmsnorm(x, g_ref[...]).astype(h_ref.dtype)
        o_ref[...] = x

        @pl.when(i % tiles_per_seq == 0)
        def _():
            carry_ref[...] = jnp.zeros(carry_ref.shape, carry_ref.dtype)

    def project(parity):
        raw, h = raws[parity], h_ref[...]
        raw[SUBLANES:, :tf] = _bf16_dot(h, wg_ref[...])
        raw[SUBLANES:, tf:] = _bf16_dot(h, wa_ref[...])

    def gate(parity):
        raw, act = raws[parity], acts[parity]
        _swap_halo(raw, carry_ref.at[j - 1], slice(None), tm)
        zero_bits, tokens = zero_ref[...], [None] * GATE_CHAINS
        for c in range(tf // LANES):
            cs = slice(c * LANES, (c + 1) * LANES)
            wts = [(cw_ref[:, cs], cb_ref[:, cs]) for cw_ref, cb_ref in ((cwg_ref, cbg_ref), (cwa_ref, cba_ref))]
            for r in range(tm // GATE_ROWS):
                r0, token = SUBLANES + r * GATE_ROWS, tokens[r % GATE_CHAINS]
                ga = [_conv3_rows(raw, r0, GATE_ROWS, slice(p * tf + c * LANES, p * tf + (c + 1) * LANES), w, token) + b
                      for p, (w, b) in enumerate(wts)]
                y = jax.nn.silu(ga[0]) * ga[1]
                tokens[r % GATE_CHAINS] = _token(y, zero_bits)
                act[r * GATE_ROWS:(r + 1) * GATE_ROWS, cs] = y.astype(act.dtype)

    def contract(parity):
        o_ref[...] += _bf16_dot(acts[parity][...], wdown_ref[...])

    def last():
        if final_norm:
            o_ref[...] = _rmsnorm(o_ref[...], fg_ref[...])

    _three_stage_step(j, nj, first, project, gate, contract, last)


def _conv_ffn_block(x, g, w_up, conv_w, conv_b, w_down, final_g, *, seq, final_norm, cast=()):
    rows, d = x.shape
    d_ff = w_down.shape[0]
    tm, tf = FFN_ROWS, FFN_TILE
    nj = d_ff // tf
    grid = (rows // tm, nj + 2)
    cb = conv_b.reshape(1, 2 * d_ff)
    cast_in, cast_out, cast_shapes, cast_counts = _cast_specs(cast, grid[0] * grid[1], grid[1])
    body = functools.partial(_conv_ffn_kernel, tiles_per_seq=seq // tm, tf=tf, nj=nj,
                             final_norm=final_norm, cast_counts=cast_counts)
    return pl.pallas_call(
        body,
        grid=grid,
        in_specs=[
            pl.BlockSpec((tm, d), _step0_rows(grid[0])),
            pl.BlockSpec((1, d), lambda i, j: (0, 0)),
            pl.BlockSpec((SUBLANES, LANES), lambda i, j: (0, 0)),
            pl.BlockSpec((d, tf), lambda i, j: (0, _stage_block(j, 0, nj))),
            pl.BlockSpec((d, tf), lambda i, j: (0, nj + _stage_block(j, 0, nj))),
            pl.BlockSpec((3, tf), lambda i, j: (0, _stage_block(j, 1, nj))),
            pl.BlockSpec((3, tf), lambda i, j: (0, nj + _stage_block(j, 1, nj))),
            pl.BlockSpec((1, tf), lambda i, j: (0, _stage_block(j, 1, nj))),
            pl.BlockSpec((1, tf), lambda i, j: (0, nj + _stage_block(j, 1, nj))),
            pl.BlockSpec((tf, d), lambda i, j: (_stage_block(j, 2, nj), 0)),
            pl.BlockSpec((1, d), lambda i, j: (0, 0)),
            *cast_in,
        ],
        out_specs=[pl.BlockSpec((tm, d), lambda i, j: (i, 0)), *cast_out],
        out_shape=[jax.ShapeDtypeStruct((rows, d), jnp.float32), *cast_shapes],
        scratch_shapes=[
            pltpu.VMEM((tm, d), jnp.bfloat16),
            pltpu.VMEM((tm + SUBLANES, 2 * tf), jnp.float32),
            pltpu.VMEM((tm + SUBLANES, 2 * tf), jnp.float32),
            pltpu.VMEM((tm, tf), jnp.bfloat16),
            pltpu.VMEM((tm, tf), jnp.bfloat16),
            pltpu.VMEM((nj, SUBLANES, 2 * tf), jnp.float32),
        ],
        compiler_params=pltpu.CompilerParams(
            dimension_semantics=("arbitrary", "arbitrary"),
            vmem_limit_bytes=VMEM_LIMIT_BYTES),
        name="conv_ffn",
    )(x, g.reshape(1, d), _zero_bits(), w_up, w_up, conv_w, conv_w, cb, cb,
      w_down, final_g.reshape(1, d), *(stack for stack, _ in cast))


def _spatial_gate_kernel(x_ref, g_ref, win_ref, vg_ref, ws_ref, bst_ref, wout_ref, o_ref,
                         h_ref, uv_ref, ug_ref, *, groups):
    n_in, tm, tc = uv_ref.shape
    width = wout_ref.shape[0]
    half_blocks = n_in // 2
    gw = width // groups
    groups_per_block = tc // gw

    h_ref[...] = _rmsnorm(x_ref[...], g_ref[...]).astype(h_ref.dtype)

    def project(c):
        uv_ref[c] = _bf16_dot(h_ref[...], win_ref[:, c * tc:(c + 1) * tc])

    def gelu(c):
        uv_ref[c] = jax.nn.gelu(uv_ref[c])

    project(0)
    for c in range(1, n_in):
        gelu(c - 1)
        project(c)
    gelu(n_in - 1)

    ssq = jnp.zeros((tm, 1), jnp.float32)
    for b in range(half_blocks):
        v = uv_ref[half_blocks + b]
        ssq += jnp.sum(v * v, axis=-1, keepdims=True)
    inv = lax.rsqrt(ssq / width + RMS_EPS)
    row = lax.broadcasted_iota(jnp.int32, (CHUNK, CHUNK), 0)
    col = lax.broadcasted_iota(jnp.int32, (CHUNK, CHUNK), 1)
    tril = (row >= col).astype(jnp.float32)
    for b in range(half_blocks):
        vn = ((uv_ref[half_blocks + b] * inv) * vg_ref[:, b * tc:(b + 1) * tc]).astype(jnp.bfloat16)
        for gl in range(groups_per_block):
            hg = b * groups_per_block + gl
            ws = (ws_ref[hg] * tril).astype(jnp.bfloat16)
            bias = bst_ref[:, hg:hg + 1]
            for c in range(tm // CHUNK):
                rs = slice(c * CHUNK, (c + 1) * CHUNK)
                gate = _bf16_dot(ws, vn[rs, gl * gw:(gl + 1) * gw]) + bias
                u = uv_ref[b, rs, gl * gw:(gl + 1) * gw]
                ug_ref[rs, hg * gw:(hg + 1) * gw] = (u * gate).astype(ug_ref.dtype)

    for k in range(o_ref.shape[1] // tc):
        cs = slice(k * tc, (k + 1) * tc)
        o_ref[:, cs] = x_ref[:, cs] + _bf16_dot(ug_ref[...], wout_ref[:, cs])


def _spatial_gate_block(x, g, w_in, v_norm, w_s, b_s, w_out):
    rows, d = x.shape
    width = w_out.shape[0]
    groups = w_s.shape[0]
    tm, tc = SG_ROWS, SG_TILE
    resident = dict(pipeline_mode=pl.Buffered(1))
    body = functools.partial(_spatial_gate_kernel, groups=groups)
    return pl.pallas_call(
        body,
        grid=(rows // tm,),
        in_specs=[
            pl.BlockSpec((tm, d), lambda i: (i, 0)),
            pl.BlockSpec((1, d), lambda i: (0, 0)),
            pl.BlockSpec((d, 2 * width), lambda i: (0, 0), **resident),
            pl.BlockSpec((1, width), lambda i: (0, 0)),
            pl.BlockSpec((groups, CHUNK, CHUNK), lambda i: (0, 0, 0)),
            pl.BlockSpec((CHUNK, groups), lambda i: (0, 0)),
            pl.BlockSpec((width, d), lambda i: (0, 0), **resident),
        ],
        out_specs=pl.BlockSpec((tm, d), lambda i: (i, 0)),
        out_shape=jax.ShapeDtypeStruct((rows, d), jnp.float32),
        scratch_shapes=[
            pltpu.VMEM((tm, d), jnp.bfloat16),
            pltpu.VMEM((2 * width // tc, tm, tc), jnp.float32),
            pltpu.VMEM((tm, width), jnp.bfloat16),
        ],
        compiler_params=pltpu.CompilerParams(
            dimension_semantics=("arbitrary",),
            vmem_limit_bytes=VMEM_LIMIT_BYTES),
        name="spatial_gate_mixer",
    )(x, g.reshape(1, d), w_in, v_norm.reshape(1, width), w_s, b_s.T, w_out)


def kernel(x, a_norm, a_in, a_conv, a_out, b_norm, b_in, b_vnorm, b_ws, b_bs, b_out,
           f_norm, f_up, f_conv_w, f_conv_b, f_down, final_norm):
    batch, seq, d = x.shape
    depth = f_norm.shape[0]
    assert seq % FFN_ROWS == 0 and seq % CONV_ROWS == 0 and seq % SG_ROWS == 0 and SG_ROWS % CHUNK == 0

    def mixer_weights(layer):
        m = layer // 2
        return ((a_in, m), (a_out, m)) if layer % 2 == 0 else ((b_in, m), (b_out, m))

    h = x.reshape(batch * seq, d)
    w_mix = tuple(stack[k].astype(jnp.bfloat16) for stack, k in mixer_weights(0))
    w_ffn = ()
    for layer in range(depth):
        m, nxt = layer // 2, layer + 1
        to_cast = () if w_ffn else ((f_up, layer), (f_down, layer))
        if layer % 2 == 0:
            h, *done = _short_conv_block(h, a_norm[m], w_mix[0], a_conv[m], w_mix[1], seq=seq, cast=to_cast)
            w_ffn = w_ffn or tuple(done)
        else:
            assert not to_cast
            h = _spatial_gate_block(h, b_norm[m], w_mix[0], b_vnorm[m], b_ws[m], b_bs[m], w_mix[1])
        to_cast = (*mixer_weights(nxt), (f_up, nxt), (f_down, nxt)) if nxt < depth else ()
        h, *done = _conv_ffn_block(h, f_norm[layer], w_ffn[0], f_conv_w[layer], f_conv_b[layer], w_ffn[1],
                                   final_norm, seq=seq, final_norm=(nxt == depth), cast=to_cast)
        w_mix, w_ffn = tuple(done[:2]), tuple(done[2:])
    return h.reshape(batch, seq, d)
```

```python
import functools

import jax
import jax.numpy as jnp
from jax import lax
from jax.experimental import pallas as pl
from jax.experimental.pallas import tpu as pltpu

RMS_EPS = 1e-5
CHUNK = 128
SUBLANES = 8
LANES = 128
FFN_ROWS = 1024
FFN_TILE = 256
CONV_ROWS = 1024
CONV_TILE = 256
SG_ROWS = 512
SG_TILE = 2048
GATE_ROWS = 16
GATE_CHAINS = 2
VMEM_LIMIT_BYTES = 58 * 1024 * 1024


def _rmsnorm(x, g):
    inv = lax.rsqrt(jnp.mean(x * x, axis=-1, keepdims=True) + RMS_EPS)
    return (x * inv) * g


def _bf16_dot(a, b):
    return jnp.dot(a, b, preferred_element_type=jnp.float32)


def _panel_rows(rows, n_steps):
    return next(pr for pr in range(2 * SUBLANES, rows + 1, 2 * SUBLANES)
                if rows % pr == 0 and rows // pr <= n_steps)


def _cast_specs(weights, n_steps, steps_per_tile):
    in_specs, out_specs, shapes, counts = [], [], [], []
    for stack, layer in weights:
        _, rows, cols = stack.shape
        pr = _panel_rows(rows, n_steps)
        n = rows // pr

        def panel(i, j, n=n):
            return jnp.minimum(i * steps_per_tile + j, n - 1)

        in_specs.append(pl.BlockSpec((None, pr, cols), lambda i, j, layer=layer, panel=panel: (layer, panel(i, j), 0)))
        out_specs.append(pl.BlockSpec((pr, cols), lambda i, j, panel=panel: (panel(i, j), 0)))
        shapes.append(jax.ShapeDtypeStruct((rows, cols), jnp.bfloat16))
        counts.append(n)
    return in_specs, out_specs, shapes, tuple(counts)


def _cast_panels(srcs, dsts, counts):
    step = pl.program_id(0) * pl.num_programs(1) + pl.program_id(1)
    for src, dst, n in zip(srcs, dsts, counts):
        @pl.when(step < n)
        def _(src=src, dst=dst):
            dst[...] = src[...].astype(dst.dtype)


def _take(refs, *counts):
    it = iter(refs)
    return [tuple(next(it) for _ in range(c)) for c in counts]


def _three_stage_step(j, nj, first, project, gate, contract, last):
    def step(parity, do_project, do_gate, do_contract):
        if do_gate:
            gate(1 - parity)
        if do_project:
            project(parity)
        if do_contract:
            contract(parity)

    @pl.when(j == 0)
    def _():
        first()
        step(0, True, False, False)

    @pl.when(j == 1)
    def _():
        step(1, True, True, False)

    for parity in range(2):
        @pl.when((j >= 2) & (j < nj) & (j % 2 == parity))
        def _(parity=parity):
            step(parity, True, True, True)

    @pl.when(j == nj)
    def _():
        step(nj % 2, False, True, True)

    @pl.when(j == nj + 1)
    def _():
        step((nj + 1) % 2, False, False, True)
        last()


def _step0_rows(n_tiles):
    return lambda i, j: (jnp.minimum(i + (j >= 2), n_tiles - 1), 0)


def _stage_block(j, lag, nj):
    return jnp.clip(j - lag, 0, nj - 1)


def _swap_halo(raw, tail, cols, tm):
    raw[:SUBLANES, cols] = tail[...]
    tail[...] = raw[tm:, cols]


def _conv3_rows(raw, r0, rb, cols, w, token):
    win = _after(raw[r0 - SUBLANES:r0 + rb, cols], token)
    lo = SUBLANES
    return win[lo - 2:lo - 2 + rb] * w[0:1] + win[lo - 1:lo - 1 + rb] * w[1:2] + win[lo:] * w[2:3]


def _zero_bits():
    return jnp.zeros((SUBLANES, LANES), jnp.int32)


def _token(y, zero_bits):
    bits = lax.bitcast_convert_type(y[:SUBLANES], jnp.int32) & zero_bits
    return lax.bitcast_convert_type(bits, jnp.float32)


def _after(x, token):
    if token is None:
        return x
    return x + jnp.concatenate([token] * (x.shape[0] // SUBLANES), axis=0)


def _short_conv_kernel(*refs, tiles_per_seq, tn, nj, cast_counts):
    nc = len(cast_counts)
    ((x_ref, g_ref, zero_ref, wb_ref, wc_ref, wx_ref, cw_ref, wout_ref), cast_srcs, (o_ref,), cast_dsts,
     (h_ref, raw_even, raw_odd, act_even, act_odd, carry_ref)) = _take(refs, 8, nc, 1, nc, 6)
    _cast_panels(cast_srcs, cast_dsts, cast_counts)
    i, j = pl.program_id(0), pl.program_id(1)
    raws, acts = (raw_even, raw_odd), (act_even, act_odd)
    tm = h_ref.shape[0]
    p_cols = slice(tn, 2 * tn)

    def first():
        x = x_ref[...]
        h_ref[...] = _rmsnorm(x, g_ref[...]).astype(h_ref.dtype)
        o_ref[...] = x

        @pl.when(i % tiles_per_seq == 0)
        def _():
            carry_ref[...] = jnp.zeros(carry_ref.shape, carry_ref.dtype)

    def project(parity):
        raw, h = raws[parity], h_ref[...]
        raw[SUBLANES:, :tn] = _bf16_dot(h, wb_ref[...])
        raw[SUBLANES:, p_cols] = _bf16_dot(h, wc_ref[...]) * _bf16_dot(h, wx_ref[...])

    def gate(parity):
        raw, act = raws[parity], acts[parity]
        _swap_halo(raw, carry_ref.at[j - 1], p_cols, tm)
        zero_bits, tokens = zero_ref[...], [None] * GATE_CHAINS
        for c in range(tn // LANES):
            cs = slice(c * LANES, (c + 1) * LANES)
            w = cw_ref[:, cs]
            for r in range(tm // GATE_ROWS):
                r0, token = SUBLANES + r * GATE_ROWS, tokens[r % GATE_CHAINS]
                conv = _conv3_rows(raw, r0, GATE_ROWS, slice(tn + c * LANES, tn + (c + 1) * LANES), w, token)
                y = _after(raw[r0:r0 + GATE_ROWS, cs], token) * conv
                tokens[r % GATE_CHAINS] = _token(y, zero_bits)
                act[r * GATE_ROWS:(r + 1) * GATE_ROWS, cs] = y.astype(act.dtype)

    def contract(parity):
        o_ref[...] += _bf16_dot(acts[parity][...], wout_ref[...])

    _three_stage_step(j, nj, first, project, gate, contract, lambda: None)


def _short_conv_block(x, g, w_in, w_conv, w_out, *, seq, cast=()):
    rows, d = x.shape
    tm, tn = CONV_ROWS, CONV_TILE
    nj = d // tn
    grid = (rows // tm, nj + 2)
    cast_in, cast_out, cast_shapes, cast_counts = _cast_specs(cast, grid[0] * grid[1], grid[1])
    body = functools.partial(_short_conv_kernel, tiles_per_seq=seq // tm, tn=tn, nj=nj,
                             cast_counts=cast_counts)
    return pl.pallas_call(
        body,
        grid=grid,
        in_specs=[
            pl.BlockSpec((tm, d), _step0_rows(grid[0])),
            pl.BlockSpec((1, d), lambda i, j: (0, 0)),
            pl.BlockSpec((SUBLANES, LANES), lambda i, j: (0, 0)),
            pl.BlockSpec((d, tn), lambda i, j: (0, _stage_block(j, 0, nj))),
            pl.BlockSpec((d, tn), lambda i, j: (0, nj + _stage_block(j, 0, nj))),
            pl.BlockSpec((d, tn), lambda i, j: (0, 2 * nj + _stage_block(j, 0, nj))),
            pl.BlockSpec((3, tn), lambda i, j: (0, _stage_block(j, 1, nj))),
            pl.BlockSpec((tn, d), lambda i, j: (_stage_block(j, 2, nj), 0)),
            *cast_in,
        ],
        out_specs=[pl.BlockSpec((tm, d), lambda i, j: (i, 0)), *cast_out],
        out_shape=[jax.ShapeDtypeStruct((rows, d), jnp.float32), *cast_shapes],
        scratch_shapes=[
            pltpu.VMEM((tm, d), jnp.bfloat16),
            pltpu.VMEM((tm + SUBLANES, 2 * tn), jnp.float32),
            pltpu.VMEM((tm + SUBLANES, 2 * tn), jnp.float32),
            pltpu.VMEM((tm, tn), jnp.bfloat16),
            pltpu.VMEM((tm, tn), jnp.bfloat16),
            pltpu.VMEM((nj, SUBLANES, tn), jnp.float32),
        ],
        compiler_params=pltpu.CompilerParams(
            dimension_semantics=("arbitrary", "arbitrary"),
            vmem_limit_bytes=VMEM_LIMIT_BYTES),
        name="short_conv_mixer",
    )(x, g.reshape(1, d), _zero_bits(), w_in, w_in, w_in, w_conv, w_out, *(stack for stack, _ in cast))


def _conv_ffn_kernel(*refs, tiles_per_seq, tf, nj, final_norm, cast_counts):
    nc = len(cast_counts)
    ((x_ref, g_ref, zero_ref, wg_ref, wa_ref, cwg_ref, cwa_ref, cbg_ref, cba_ref, wdown_ref, fg_ref),
     cast_srcs, (o_ref,), cast_dsts, (h_ref, raw_even, raw_odd, act_even, act_odd, carry_ref)) = _take(
        refs, 11, nc, 1, nc, 6)
    _cast_panels(cast_srcs, cast_dsts, cast_counts)
    i, j = pl.program_id(0), pl.program_id(1)
    raws, acts = (raw_even, raw_odd), (act_even, act_odd)
    tm = h_ref.shape[0]

    def first():
        x = x_ref[...]
        h_ref[...] = _rmsnorm(x, g_ref[...]).astype(h_ref.dtype)
        o_ref[...] = x

        @pl.when(i % tiles_per_seq == 0)
        def _():
            carry_ref[...] = jnp.zeros(carry_ref.shape, carry_ref.dtype)

    def project(parity):
        raw, h = raws[parity], h_ref[...]
        raw[SUBLANES:, :tf] = _bf16_dot(h, wg_ref[...])
        raw[SUBLANES:, tf:] = _bf16_dot(h, wa_ref[...])

    def gate(parity):
        raw, act = raws[parity], acts[parity]
        _swap_halo(raw, carry_ref.at[j - 1], slice(None), tm)
        zero_bits, tokens = zero_ref[...], [None] * GATE_CHAINS
        for c in range(tf // LANES):
            cs = slice(c * LANES, (c + 1) * LANES)
            wts = [(cw_ref[:, cs], cb_ref[:, cs]) for cw_ref, cb_ref in ((cwg_ref, cbg_ref), (cwa_ref, cba_ref))]
            for r in range(tm // GATE_ROWS):
                r0, token = SUBLANES + r * GATE_ROWS, tokens[r % GATE_CHAINS]
                ga = [_conv3_rows(raw, r0, GATE_ROWS, slice(p * tf + c * LANES, p * tf + (c + 1) * LANES), w, token) + b
                      for p, (w, b) in enumerate(wts)]
                y = jax.nn.silu(ga[0]) * ga[1]
                tokens[r % GATE_CHAINS] = _token(y, zero_bits)
                act[r * GATE_ROWS:(r + 1) * GATE_ROWS, cs] = y.astype(act.dtype)

    def contract(parity):
        o_ref[...] += _bf16_dot(acts[parity][...], wdown_ref[...])

    def last():
        if final_norm:
            o_ref[...] = _rmsnorm(o_ref[...], fg_ref[...])

    _three_stage_step(j, nj, first, project, gate, contract, last)


def _conv_ffn_block(x, g, w_up, conv_w, conv_b, w_down, final_g, *, seq, final_norm, cast=()):
    rows, d = x.shape
    d_ff = w_down.shape[0]
    tm, tf = FFN_ROWS, FFN_TILE
    nj = d_ff // tf
    grid = (rows // tm, nj + 2)
    cb = conv_b.reshape(1, 2 * d_ff)
    cast_in, cast_out, cast_shapes, cast_counts = _cast_specs(cast, grid[0] * grid[1], grid[1])
    body = functools.partial(_conv_ffn_kernel, tiles_per_seq=seq // tm, tf=tf, nj=nj,
                             final_norm=final_norm, cast_counts=cast_counts)
    return pl.pallas_call(
        body,
        grid=grid,
        in_specs=[
            pl.BlockSpec((tm, d), _step0_rows(grid[0])),
            pl.BlockSpec((1, d), lambda i, j: (0, 0)),
            pl.BlockSpec((SUBLANES, LANES), lambda i, j: (0, 0)),
            pl.BlockSpec((d, tf), lambda i, j: (0, _stage_block(j, 0, nj))),
            pl.BlockSpec((d, tf), lambda i, j: (0, nj + _stage_block(j, 0, nj))),
            pl.BlockSpec((3, tf), lambda i, j: (0, _stage_block(j, 1, nj))),
            pl.BlockSpec((3, tf), lambda i, j: (0, nj + _stage_block(j, 1, nj))),
            pl.BlockSpec((1, tf), lambda i, j: (0, _stage_block(j, 1, nj))),
            pl.BlockSpec((1, tf), lambda i, j: (0, nj + _stage_block(j, 1, nj))),
            pl.BlockSpec((tf, d), lambda i, j: (_stage_block(j, 2, nj), 0)),
            pl.BlockSpec((1, d), lambda i, j: (0, 0)),
            *cast_in,
        ],
        out_specs=[pl.BlockSpec((tm, d), lambda i, j: (i, 0)), *cast_out],
        out_shape=[jax.ShapeDtypeStruct((rows, d), jnp.float32), *cast_shapes],
        scratch_shapes=[
            pltpu.VMEM((tm, d), jnp.bfloat16),
            pltpu.VMEM((tm + SUBLANES, 2 * tf), jnp.float32),
            pltpu.VMEM((tm + SUBLANES, 2 * tf), jnp.float32),
            pltpu.VMEM((tm, tf), jnp.bfloat16),
            pltpu.VMEM((tm, tf), jnp.bfloat16),
            pltpu.VMEM((nj, SUBLANES, 2 * tf), jnp.float32),
        ],
        compiler_params=pltpu.CompilerParams(
            dimension_semantics=("arbitrary", "arbitrary"),
            vmem_limit_bytes=VMEM_LIMIT_BYTES),
        name="conv_ffn",
    )(x, g.reshape(1, d), _zero_bits(), w_up, w_up, conv_w, conv_w, cb, cb,
      w_down, final_g.reshape(1, d), *(stack for stack, _ in cast))


def _spatial_gate_kernel(x_ref, g_ref, win_ref, vg_ref, ws_ref, bst_ref, wout_ref, o_ref,
                         h_ref, uv_ref, ug_ref, *, groups):
    n_in, tm, tc = uv_ref.shape
    width = wout_ref.shape[0]
    half_blocks = n_in // 2
    gw = width // groups
    groups_per_block = tc // gw

    h_ref[...] = _rmsnorm(x_ref[...], g_ref[...]).astype(h_ref.dtype)

    def project(c):
        uv_ref[c] = _bf16_dot(h_ref[...], win_ref[:, c * tc:(c + 1) * tc])

    def gelu(c):
        uv_ref[c] = jax.nn.gelu(uv_ref[c])

    project(0)
    for c in range(1, n_in):
        gelu(c - 1)
        project(c)
    gelu(n_in - 1)

    ssq = jnp.zeros((tm, 1), jnp.float32)
    for b in range(half_blocks):
        v = uv_ref[half_blocks + b]
        ssq += jnp.sum(v * v, axis=-1, keepdims=True)
    inv = lax.rsqrt(ssq / width + RMS_EPS)
    row = lax.broadcasted_iota(jnp.int32, (CHUNK, CHUNK), 0)
    col = lax.broadcasted_iota(jnp.int32, (CHUNK, CHUNK), 1)
    tril = (row >= col).astype(jnp.float32)
    for b in range(half_blocks):
        vn = ((uv_ref[half_blocks + b] * inv) * vg_ref[:, b * tc:(b + 1) * tc]).astype(jnp.bfloat16)
        for gl in range(groups_per_block):
            hg = b * groups_per_block + gl
            ws = (ws_ref[hg] * tril).astype(jnp.bfloat16)
            bias = bst_ref[:, hg:hg + 1]
            for c in range(tm // CHUNK):
                rs = slice(c * CHUNK, (c + 1) * CHUNK)
                gate = _bf16_dot(ws, vn[rs, gl * gw:(gl + 1) * gw]) + bias
                u = uv_ref[b, rs, gl * gw:(gl + 1) * gw]
                ug_ref[rs, hg * gw:(hg + 1) * gw] = (u * gate).astype(ug_ref.dtype)

    for k in range(o_ref.shape[1] // tc):
        cs = slice(k * tc, (k + 1) * tc)
        o_ref[:, cs] = x_ref[:, cs] + _bf16_dot(ug_ref[...], wout_ref[:, cs])


def _spatial_gate_block(x, g, w_in, v_norm, w_s, b_s, w_out):
    rows, d = x.shape
    width = w_out.shape[0]
    groups = w_s.shape[0]
    tm, tc = SG_ROWS, SG_TILE
    resident = dict(pipeline_mode=pl.Buffered(1))
    body = functools.partial(_spatial_gate_kernel, groups=groups)
    return pl.pallas_call(
        body,
        grid=(rows // tm,),
        in_specs=[
            pl.BlockSpec((tm, d), lambda i: (i, 0)),
            pl.BlockSpec((1, d), lambda i: (0, 0)),
            pl.BlockSpec((d, 2 * width), lambda i: (0, 0), **resident),
            pl.BlockSpec((1, width), lambda i: (0, 0)),
            pl.BlockSpec((groups, CHUNK, CHUNK), lambda i: (0, 0, 0)),
            pl.BlockSpec((CHUNK, groups), lambda i: (0, 0)),
            pl.BlockSpec((width, d), lambda i: (0, 0), **resident),
        ],
        out_specs=pl.BlockSpec((tm, d), lambda i: (i, 0)),
        out_shape=jax.ShapeDtypeStruct((rows, d), jnp.float32),
        scratch_shapes=[
            pltpu.VMEM((tm, d), jnp.bfloat16),
            pltpu.VMEM((2 * width // tc, tm, tc), jnp.float32),
            pltpu.VMEM((tm, width), jnp.bfloat16),
        ],
        compiler_params=pltpu.CompilerParams(
            dimension_semantics=("arbitrary",),
            vmem_limit_bytes=VMEM_LIMIT_BYTES),
        name="spatial_gate_mixer",
    )(x, g.reshape(1, d), w_in, v_norm.reshape(1, width), w_s, b_s.T, w_out)


def kernel(x, a_norm, a_in, a_conv, a_out, b_norm, b_in, b_vnorm, b_ws, b_bs, b_out,
           f_norm, f_up, f_conv_w, f_conv_b, f_down, final_norm):
    batch, seq, d = x.shape
    depth = f_norm.shape[0]
    assert seq % FFN_ROWS == 0 and seq % CONV_ROWS == 0 and seq % SG_ROWS == 0 and SG_ROWS % CHUNK == 0

    def mixer_weights(layer):
        m = layer // 2
        return ((a_in, m), (a_out, m)) if layer % 2 == 0 else ((b_in, m), (b_out, m))

    h = x.reshape(batch * seq, d)
    w_mix = tuple(stack[k].astype(jnp.bfloat16) for stack, k in mixer_weights(0))
    w_ffn = ()
    for layer in range(depth):
        m, nxt = layer // 2, layer + 1
        to_cast = () if w_ffn else ((f_up, layer), (f_down, layer))
        if layer % 2 == 0:
            h, *done = _short_conv_block(h, a_norm[m], w_mix[0], a_conv[m], w_mix[1], seq=seq, cast=to_cast)
            w_ffn = w_ffn or tuple(done)
        else:
            assert not to_cast
            h = _spatial_gate_block(h, b_norm[m], w_mix[0], b_vnorm[m], b_ws[m], b_bs[m], w_mix[1])
        to_cast = (*mixer_weights(nxt), (f_up, nxt), (f_down, nxt)) if nxt < depth else ()
        h, *done = _conv_ffn_block(h, f_norm[layer], w_ffn[0], f_conv_w[layer], f_conv_b[layer], w_ffn[1],
                                   final_norm, seq=seq, final_norm=(nxt == depth), cast=to_cast)
        w_mix, w_ffn = tuple(done[:2]), tuple(done[2:])
    return h.reshape(batch, seq, d)
```

```python
import functools

import jax
import jax.numpy as jnp
from jax import lax
from jax.experimental import pallas as pl
from jax.experimental.pallas import tpu as pltpu

RMS_EPS = 1e-5
CHUNK = 128
SUBLANES = 8
LANES = 128
FFN_ROWS = 1024
FFN_TILE = 256
CONV_ROWS = 1024
CONV_TILE = 256
SG_ROWS = 512
SG_TILE = 2048
GATE_ROWS = 16
X_NEXT_STEP = 5
GATE_CHAINS = 2
VMEM_LIMIT_BYTES = 58 * 1024 * 1024


def _rmsnorm(x, g):
    inv = lax.rsqrt(jnp.mean(x * x, axis=-1, keepdims=True) + RMS_EPS)
    return (x * inv) * g


def _bf16_dot(a, b):
    return jnp.dot(a, b, preferred_element_type=jnp.float32)


def _panel_rows(rows, n_steps):
    return next(pr for pr in range(2 * SUBLANES, rows + 1, 2 * SUBLANES)
                if rows % pr == 0 and rows // pr <= n_steps)


def _cast_specs(weights, n_steps, steps_per_tile):
    in_specs, out_specs, shapes, counts = [], [], [], []
    for stack, layer in weights:
        _, rows, cols = stack.shape
        pr = _panel_rows(rows, n_steps)
        n = rows // pr

        def panel(i, j, n=n):
            return jnp.minimum(i * steps_per_tile + j, n - 1)

        in_specs.append(pl.BlockSpec((None, pr, cols), lambda i, j, layer=layer, panel=panel: (layer, panel(i, j), 0)))
        out_specs.append(pl.BlockSpec((pr, cols), lambda i, j, panel=panel: (panel(i, j), 0)))
        shapes.append(jax.ShapeDtypeStruct((rows, cols), jnp.bfloat16))
        counts.append(n)
    return in_specs, out_specs, shapes, tuple(counts)


def _cast_panels(srcs, dsts, counts):
    step = pl.program_id(0) * pl.num_programs(1) + pl.program_id(1)
    for src, dst, n in zip(srcs, dsts, counts):
        @pl.when(step < n)
        def _(src=src, dst=dst):
            dst[...] = src[...].astype(dst.dtype)


def _take(refs, *counts):
    it = iter(refs)
    return [tuple(next(it) for _ in range(c)) for c in counts]


def _three_stage_step(j, nj, first, project, gate, contract, last):
    def step(parity, do_project, do_gate, do_contract):
        if do_gate:
            gate(1 - parity)
        if do_project:
            project(parity)
        if do_contract:
            contract(parity)

    @pl.when(j == 0)
    def _():
        first()
        step(0, True, False, False)

    @pl.when(j == 1)
    def _():
        step(1, True, True, False)

    for parity in range(2):
        @pl.when((j >= 2) & (j < nj) & (j % 2 == parity))
        def _(parity=parity):
            step(parity, True, True, True)

    @pl.when(j == nj)
    def _():
        step(nj % 2, False, True, True)

    @pl.when(j == nj + 1)
    def _():
        step((nj + 1) % 2, False, False, True)
        last()


def _step0_rows(n_tiles):
    return lambda i, j: (jnp.minimum(i + (j >= X_NEXT_STEP), n_tiles - 1), 0)


def _stage_block(j, lag, nj):
    return jnp.clip(j - lag, 0, nj - 1)


def _swap_halo(raw, tail, cols, tm):
    raw[:SUBLANES, cols] = tail[...]
    tail[...] = raw[tm:, cols]


def _conv3_rows(raw, r0, rb, cols, w, token):
    win = _after(raw[r0 - SUBLANES:r0 + rb, cols], token)
    lo = SUBLANES
    return win[lo - 2:lo - 2 + rb] * w[0:1] + win[lo - 1:lo - 1 + rb] * w[1:2] + win[lo:] * w[2:3]


def _zero_bits():
    return jnp.zeros((SUBLANES, LANES), jnp.int32)


def _token(y, zero_bits):
    bits = lax.bitcast_convert_type(y[:SUBLANES], jnp.int32) & zero_bits
    return lax.bitcast_convert_type(bits, jnp.float32)


def _after(x, token):
    if token is None:
        return x
    return x + jnp.concatenate([token] * (x.shape[0] // SUBLANES), axis=0)


def _short_conv_kernel(*refs, tiles_per_seq, tn, nj, cast_counts):
    nc = len(cast_counts)
    ((x_ref, g_ref, zero_ref, wb_ref, wc_ref, wx_ref, cw_ref, wout_ref), cast_srcs, (o_ref,), cast_dsts,
     (h_ref, raw_even, raw_odd, act_even, act_odd, carry_ref)) = _take(refs, 8, nc, 1, nc, 6)
    _cast_panels(cast_srcs, cast_dsts, cast_counts)
    i, j = pl.program_id(0), pl.program_id(1)
    raws, acts = (raw_even, raw_odd), (act_even, act_odd)
    tm = h_ref.shape[0]
    p_cols = slice(tn, 2 * tn)

    def first():
        x = x_ref[...]
        h_ref[...] = _rmsnorm(x, g_ref[...]).astype(h_ref.dtype)
        o_ref[...] = x

        @pl.when(i % tiles_per_seq == 0)
        def _():
            carry_ref[...] = jnp.zeros(carry_ref.shape, carry_ref.dtype)

    def project(parity):
        raw, h = raws[parity], h_ref[...]
        raw[SUBLANES:, :tn] = _bf16_dot(h, wb_ref[...])
        raw[SUBLANES:, p_cols] = _bf16_dot(h, wc_ref[...]) * _bf16_dot(h, wx_ref[...])

    def gate(parity):
        raw, act = raws[parity], acts[parity]
        _swap_halo(raw, carry_ref.at[j - 1], p_cols, tm)
        zero_bits, tokens = zero_ref[...], [None] * GATE_CHAINS
        for c in range(tn // LANES):
            cs = slice(c * LANES, (c + 1) * LANES)
            w = cw_ref[:, cs]
            for r in range(tm // GATE_ROWS):
                r0, token = SUBLANES + r * GATE_ROWS, tokens[r % GATE_CHAINS]
                conv = _conv3_rows(raw, r0, GATE_ROWS, slice(tn + c * LANES, tn + (c + 1) * LANES), w, token)
                y = _after(raw[r0:r0 + GATE_ROWS, cs], token) * conv
                tokens[r % GATE_CHAINS] = _token(y, zero_bits)
                act[r * GATE_ROWS:(r + 1) * GATE_ROWS, cs] = y.astype(act.dtype)

    def contract(parity):
        o_ref[...] += _bf16_dot(acts[parity][...], wout_ref[...])

    _three_stage_step(j, nj, first, project, gate, contract, lambda: None)


def _short_conv_block(x, g, w_in, w_conv, w_out, *, seq, cast=()):
    rows, d = x.shape
    tm, tn = CONV_ROWS, CONV_TILE
    nj = d // tn
    grid = (rows // tm, nj + 2)
    cast_in, cast_out, cast_shapes, cast_counts = _cast_specs(cast, grid[0] * grid[1], grid[1])
    body = functools.partial(_short_conv_kernel, tiles_per_seq=seq // tm, tn=tn, nj=nj,
                             cast_counts=cast_counts)
    return pl.pallas_call(
        body,
        grid=grid,
        in_specs=[
            pl.BlockSpec((tm, d), _step0_rows(grid[0])),
            pl.BlockSpec((1, d), lambda i, j: (0, 0)),
            pl.BlockSpec((SUBLANES, LANES), lambda i, j: (0, 0)),
            pl.BlockSpec((d, tn), lambda i, j: (0, _stage_block(j, 0, nj))),
            pl.BlockSpec((d, tn), lambda i, j: (0, nj + _stage_block(j, 0, nj))),
            pl.BlockSpec((d, tn), lambda i, j: (0, 2 * nj + _stage_block(j, 0, nj))),
            pl.BlockSpec((3, tn), lambda i, j: (0, _stage_block(j, 1, nj))),
            pl.BlockSpec((tn, d), lambda i, j: (_stage_block(j, 2, nj), 0)),
            *cast_in,
        ],
        out_specs=[pl.BlockSpec((tm, d), lambda i, j: (i, 0)), *cast_out],
        out_shape=[jax.ShapeDtypeStruct((rows, d), jnp.float32), *cast_shapes],
        scratch_shapes=[
            pltpu.VMEM((tm, d), jnp.bfloat16),
            pltpu.VMEM((tm + SUBLANES, 2 * tn), jnp.float32),
            pltpu.VMEM((tm + SUBLANES, 2 * tn), jnp.float32),
            pltpu.VMEM((tm, tn), jnp.bfloat16),
            pltpu.VMEM((tm, tn), jnp.bfloat16),
            pltpu.VMEM((nj, SUBLANES, tn), jnp.float32),
        ],
        compiler_params=pltpu.CompilerParams(
            dimension_semantics=("arbitrary", "arbitrary"),
            vmem_limit_bytes=VMEM_LIMIT_BYTES),
        name="short_conv_mixer",
    )(x, g.reshape(1, d), _zero_bits(), w_in, w_in, w_in, w_conv, w_out, *(stack for stack, _ in cast))


def _conv_ffn_kernel(*refs, tiles_per_seq, tf, nj, final_norm, cast_counts):
    nc = len(cast_counts)
    ((x_ref, g_ref, zero_ref, wg_ref, wa_ref, cwg_ref, cwa_ref, cbg_ref, cba_ref, wdown_ref, fg_ref),
     cast_srcs, (o_ref,), cast_dsts, (h_ref, raw_even, raw_odd, act_even, act_odd, carry_ref)) = _take(
        refs, 11, nc, 1, nc, 6)
    _cast_panels(cast_srcs, cast_dsts, cast_counts)
    i, j = pl.program_id(0), pl.program_id(1)
    raws, acts = (raw_even, raw_odd), (act_even, act_odd)
    tm = h_ref.shape[0]

    def first():
        x = x_ref[...]
        h_ref[...] = _rmsnorm(x, g_ref[...]).astype(h_ref.dtype)
        o_ref[...] = x

        @pl.when(i % tiles_per_seq == 0)
        def _():
            carry_ref[...] = jnp.zeros(carry_ref.shape, carry_ref.dtype)

    def project(parity):
        raw, h = raws[parity], h_ref[...]
        raw[SUBLANES:, :tf] = _bf16_dot(h, wg_ref[...])
        raw[SUBLANES:, tf:] = _bf16_dot(h, wa_ref[...])

    def gate(parity):
        raw, act = raws[parity], acts[parity]
        _swap_halo(raw, carry_ref.at[j - 1], slice(None), tm)
        zero_bits, tokens = zero_ref[...], [None] * GATE_CHAINS
        for c in range(tf // LANES):
            cs = slice(c * LANES, (c + 1) * LANES)
            wts = [(cw_ref[:, cs], cb_ref[:, cs]) for cw_ref, cb_ref in ((cwg_ref, cbg_ref), (cwa_ref, cba_ref))]
            for r in range(tm // GATE_ROWS):
                r0, token = SUBLANES + r * GATE_ROWS, tokens[r % GATE_CHAINS]
                ga = [_conv3_rows(raw, r0, GATE_ROWS, slice(p * tf + c * LANES, p * tf + (c + 1) * LANES), w, token) + b
                      for p, (w, b) in enumerate(wts)]
                y = jax.nn.silu(ga[0]) * ga[1]
                tokens[r % GATE_CHAINS] = _token(y, zero_bits)
                act[r * GATE_ROWS:(r + 1) * GATE_ROWS, cs] = y.astype(act.dtype)

    def contract(parity):
        o_ref[...] += _bf16_dot(acts[parity][...], wdown_ref[...])

    def last():
        if final_norm:
            o_ref[...] = _rmsnorm(o_ref[...], fg_ref[...])

    _three_stage_step(j, nj, first, project, gate, contract, last)


def _conv_ffn_block(x, g, w_up, conv_w, conv_b, w_down, final_g, *, seq, final_norm, cast=()):
    rows, d = x.shape
    d_ff = w_down.shape[0]
    tm, tf = FFN_ROWS, FFN_TILE
    nj = d_ff // tf
    grid = (rows // tm, nj + 2)
    cb = conv_b.reshape(1, 2 * d_ff)
    cast_in, cast_out, cast_shapes, cast_counts = _cast_specs(cast, grid[0] * grid[1], grid[1])
    body = functools.partial(_conv_ffn_kernel, tiles_per_seq=seq // tm, tf=tf, nj=nj,
                             final_norm=final_norm, cast_counts=cast_counts)
    return pl.pallas_call(
        body,
        grid=grid,
        in_specs=[
            pl.BlockSpec((tm, d), _step0_rows(grid[0])),
            pl.BlockSpec((1, d), lambda i, j: (0, 0)),
            pl.BlockSpec((SUBLANES, LANES), lambda i, j: (0, 0)),
            pl.BlockSpec((d, tf), lambda i, j: (0, _stage_block(j, 0, nj))),
            pl.BlockSpec((d, tf), lambda i, j: (0, nj + _stage_block(j, 0, nj))),
            pl.BlockSpec((3, tf), lambda i, j: (0, _stage_block(j, 1, nj))),
            pl.BlockSpec((3, tf), lambda i, j: (0, nj + _stage_block(j, 1, nj))),
            pl.BlockSpec((1, tf), lambda i, j: (0, _stage_block(j, 1, nj))),
            pl.BlockSpec((1, tf), lambda i, j: (0, nj + _stage_block(j, 1, nj))),
            pl.BlockSpec((tf, d), lambda i, j: (_stage_block(j, 2, nj), 0)),
            pl.BlockSpec((1, d), lambda i, j: (0, 0)),
            *cast_in,
        ],
        out_specs=[pl.BlockSpec((tm, d), lambda i, j: (i, 0)), *cast_out],
        out_shape=[jax.ShapeDtypeStruct((rows, d), jnp.float32), *cast_shapes],
        scratch_shapes=[
            pltpu.VMEM((tm, d), jnp.bfloat16),
            pltpu.VMEM((tm + SUBLANES, 2 * tf), jnp.float32),
            pltpu.VMEM((tm + SUBLANES, 2 * tf), jnp.float32),
            pltpu.VMEM((tm, tf), jnp.bfloat16),
            pltpu.VMEM((tm, tf), jnp.bfloat16),
            pltpu.VMEM((nj, SUBLANES, 2 * tf), jnp.float32),
        ],
        compiler_params=pltpu.CompilerParams(
            dimension_semantics=("arbitrary", "arbitrary"),
            vmem_limit_bytes=VMEM_LIMIT_BYTES),
        name="conv_ffn",
    )(x, g.reshape(1, d), _zero_bits(), w_up, w_up, conv_w, conv_w, cb, cb,
      w_down, final_g.reshape(1, d), *(stack for stack, _ in cast))


def _spatial_gate_kernel(x_ref, g_ref, win_ref, vg_ref, ws_ref, bst_ref, wout_ref, o_ref,
                         h_ref, uv_ref, ug_ref, *, groups):
    n_in, tm, tc = uv_ref.shape
    width = wout_ref.shape[0]
    half_blocks = n_in // 2
    gw = width // groups
    groups_per_block = tc // gw

    h_ref[...] = _rmsnorm(x_ref[...], g_ref[...]).astype(h_ref.dtype)

    def project(c):
        uv_ref[c] = _bf16_dot(h_ref[...], win_ref[:, c * tc:(c + 1) * tc])

    def gelu(c):
        uv_ref[c] = jax.nn.gelu(uv_ref[c])

    project(0)
    for c in range(1, n_in):
        gelu(c - 1)
        project(c)
    gelu(n_in - 1)

    ssq = jnp.zeros((tm, 1), jnp.float32)
    for b in range(half_blocks):
        v = uv_ref[half_blocks + b]
        ssq += jnp.sum(v * v, axis=-1, keepdims=True)
    inv = lax.rsqrt(ssq / width + RMS_EPS)
    row = lax.broadcasted_iota(jnp.int32, (CHUNK, CHUNK), 0)
    col = lax.broadcasted_iota(jnp.int32, (CHUNK, CHUNK), 1)
    tril = (row >= col).astype(jnp.float32)
    for b in range(half_blocks):
        vn = ((uv_ref[half_blocks + b] * inv) * vg_ref[:, b * tc:(b + 1) * tc]).astype(jnp.bfloat16)
        for gl in range(groups_per_block):
            hg = b * groups_per_block + gl
            ws = (ws_ref[hg] * tril).astype(jnp.bfloat16)
            bias = bst_ref[:, hg:hg + 1]
            for c in range(tm // CHUNK):
                rs = slice(c * CHUNK, (c + 1) * CHUNK)
                gate = _bf16_dot(ws, vn[rs, gl * gw:(gl + 1) * gw]) + bias
                u = uv_ref[b, rs, gl * gw:(gl + 1) * gw]
                ug_ref[rs, hg * gw:(hg + 1) * gw] = (u * gate).astype(ug_ref.dtype)

    for k in range(o_ref.shape[1] // tc):
        cs = slice(k * tc, (k + 1) * tc)
        o_ref[:, cs] = x_ref[:, cs] + _bf16_dot(ug_ref[...], wout_ref[:, cs])


def _spatial_gate_block(x, g, w_in, v_norm, w_s, b_s, w_out):
    rows, d = x.shape
    width = w_out.shape[0]
    groups = w_s.shape[0]
    tm, tc = SG_ROWS, SG_TILE
    resident = dict(pipeline_mode=pl.Buffered(1))
    body = functools.partial(_spatial_gate_kernel, groups=groups)
    return pl.pallas_call(
        body,
        grid=(rows // tm,),
        in_specs=[
            pl.BlockSpec((tm, d), lambda i: (i, 0)),
            pl.BlockSpec((1, d), lambda i: (0, 0)),
            pl.BlockSpec((d, 2 * width), lambda i: (0, 0), **resident),
            pl.BlockSpec((1, width), lambda i: (0, 0)),
            pl.BlockSpec((groups, CHUNK, CHUNK), lambda i: (0, 0, 0)),
            pl.BlockSpec((CHUNK, groups), lambda i: (0, 0)),
            pl.BlockSpec((width, d), lambda i: (0, 0), **resident),
        ],
        out_specs=pl.BlockSpec((tm, d), lambda i: (i, 0)),
        out_shape=jax.ShapeDtypeStruct((rows, d), jnp.float32),
        scratch_shapes=[
            pltpu.VMEM((tm, d), jnp.bfloat16),
            pltpu.VMEM((2 * width // tc, tm, tc), jnp.float32),
            pltpu.VMEM((tm, width), jnp.bfloat16),
        ],
        compiler_params=pltpu.CompilerParams(
            dimension_semantics=("arbitrary",),
            vmem_limit_bytes=VMEM_LIMIT_BYTES),
        name="spatial_gate_mixer",
    )(x, g.reshape(1, d), w_in, v_norm.reshape(1, width), w_s, b_s.T, w_out)


def kernel(x, a_norm, a_in, a_conv, a_out, b_norm, b_in, b_vnorm, b_ws, b_bs, b_out,
           f_norm, f_up, f_conv_w, f_conv_b, f_down, final_norm):
    batch, seq, d = x.shape
    depth = f_norm.shape[0]
    assert seq % FFN_ROWS == 0 and seq % CONV_ROWS == 0 and seq % SG_ROWS == 0 and SG_ROWS % CHUNK == 0

    def mixer_weights(layer):
        m = layer // 2
        return ((a_in, m), (a_out, m)) if layer % 2 == 0 else ((b_in, m), (b_out, m))

    h = x.reshape(batch * seq, d)
    w_mix = tuple(stack[k].astype(jnp.bfloat16) for stack, k in mixer_weights(0))
    w_ffn = ()
    for layer in range(depth):
        m, nxt = layer // 2, layer + 1
        to_cast = () if w_ffn else ((f_up, layer), (f_down, layer))
        if layer % 2 == 0:
            h, *done = _short_conv_block(h, a_norm[m], w_mix[0], a_conv[m], w_mix[1], seq=seq, cast=to_cast)
            w_ffn = w_ffn or tuple(done)
        else:
            assert not to_cast
            h = _spatial_gate_block(h, b_norm[m], w_mix[0], b_vnorm[m], b_ws[m], b_bs[m], w_mix[1])
        to_cast = (*mixer_weights(nxt), (f_up, nxt), (f_down, nxt)) if nxt < depth else ()
        h, *done = _conv_ffn_block(h, f_norm[layer], w_ffn[0], f_conv_w[layer], f_conv_b[layer], w_ffn[1],
                                   final_norm, seq=seq, final_norm=(nxt == depth), cast=to_cast)
        w_mix, w_ffn = tuple(done[:2]), tuple(done[2:])
    return h.reshape(batch, seq, d)
```

```python
import functools

import jax
import jax.numpy as jnp
from jax import lax
from jax.experimental import pallas as pl
from jax.experimental.pallas import tpu as pltpu

RMS_EPS = 1e-5
CHUNK = 128
SUBLANES = 8
LANES = 128
FFN_ROWS = 1024
FFN_TILE = 256
CONV_ROWS = 1024
CONV_TILE = 256
SG_ROWS = 512
SG_TILE = 2048
GATE_ROWS = 16
X_NEXT_STEP = 5
GATE_CHAINS = 2
VMEM_LIMIT_BYTES = 58 * 1024 * 1024


def _rmsnorm(x, g):
    inv = lax.rsqrt(jnp.mean(x * x, axis=-1, keepdims=True) + RMS_EPS)
    return (x * inv) * g


def _bf16_dot(a, b):
    return jnp.dot(a, b, preferred_element_type=jnp.float32)


def _panel_rows(rows, n_steps):
    return next(pr for pr in range(2 * SUBLANES, rows + 1, 2 * SUBLANES)
                if rows % pr == 0 and rows // pr <= n_steps)


def _cast_specs(weights, n_steps, steps_per_tile):
    in_specs, out_specs, shapes, counts = [], [], [], []
    for stack, layer in weights:
        _, rows, cols = stack.shape
        pr = _panel_rows(rows, n_steps)
        n = rows // pr

        def panel(i, j, n=n):
            return jnp.minimum(i * steps_per_tile + j, n - 1)

        in_specs.append(pl.BlockSpec((None, pr, cols), lambda i, j, layer=layer, panel=panel: (layer, panel(i, j), 0)))
        out_specs.append(pl.BlockSpec((pr, cols), lambda i, j, panel=panel: (panel(i, j), 0)))
        shapes.append(jax.ShapeDtypeStruct((rows, cols), jnp.bfloat16))
        counts.append(n)
    return in_specs, out_specs, shapes, tuple(counts)


def _cast_panels(srcs, dsts, counts):
    step = pl.program_id(0) * pl.num_programs(1) + pl.program_id(1)
    for src, dst, n in zip(srcs, dsts, counts):
        @pl.when(step < n)
        def _(src=src, dst=dst):
            dst[...] = src[...].astype(dst.dtype)


def _take(refs, *counts):
    it = iter(refs)
    return [tuple(next(it) for _ in range(c)) for c in counts]


def _three_stage_step(j, nj, first, project, gate, contract, last):
    def step(parity, do_project, do_gate, do_contract):
        if do_gate:
            gate(1 - parity)
        if do_project:
            project(parity)
        if do_contract:
            contract(parity)

    @pl.when(j == 0)
    def _():
        first()
        step(0, True, False, False)

    @pl.when(j == 1)
    def _():
        step(1, True, True, False)

    for parity in range(2):
        @pl.when((j >= 2) & (j < nj) & (j % 2 == parity))
        def _(parity=parity):
            step(parity, True, True, True)

    @pl.when(j == nj)
    def _():
        step(nj % 2, False, True, True)

    @pl.when(j == nj + 1)
    def _():
        step((nj + 1) % 2, False, False, True)
        last()


def _step0_rows(n_tiles):
    return lambda i, j: (jnp.minimum(i + (j >= X_NEXT_STEP), n_tiles - 1), 0)


def _stage_block(j, lag, nj):
    return jnp.clip(j - lag, 0, nj - 1)


def _swap_halo(raw, tail, cols, tm):
    raw[:SUBLANES, cols] = tail[...]
    tail[...] = raw[tm:, cols]


def _conv3_rows(raw, r0, rb, cols, w, token):
    win = _after(raw[r0 - SUBLANES:r0 + rb, cols], token)
    lo = SUBLANES
    return win[lo - 2:lo - 2 + rb] * w[0:1] + win[lo - 1:lo - 1 + rb] * w[1:2] + win[lo:] * w[2:3]


def _zero_bits():
    return jnp.zeros((SUBLANES, LANES), jnp.int32)


def _token(y, zero_bits):
    bits = lax.bitcast_convert_type(y[:SUBLANES], jnp.int32) & zero_bits
    return lax.bitcast_convert_type(bits, jnp.float32)


def _after(x, token):
    if token is None:
        return x
    return x + jnp.concatenate([token] * (x.shape[0] // SUBLANES), axis=0)


def _short_conv_kernel(*refs, tiles_per_seq, tn, nj, cast_counts):
    nc = len(cast_counts)
    ((x_ref, g_ref, zero_ref, wb_ref, wc_ref, wx_ref, cw_ref, wout_ref), cast_srcs, (o_ref,), cast_dsts,
     (h_ref, raw_even, raw_odd, act_even, act_odd, carry_ref)) = _take(refs, 8, nc, 1, nc, 6)
    _cast_panels(cast_srcs, cast_dsts, cast_counts)
    i, j = pl.program_id(0), pl.program_id(1)
    raws, acts = (raw_even, raw_odd), (act_even, act_odd)
    tm = h_ref.shape[0]
    p_cols = slice(tn, 2 * tn)

    def first():
        x = x_ref[...]
        h_ref[...] = _rmsnorm(x, g_ref[...]).astype(h_ref.dtype)
        o_ref[...] = x

        @pl.when(i % tiles_per_seq == 0)
        def _():
            carry_ref[...] = jnp.zeros(carry_ref.shape, carry_ref.dtype)

    def project(parity):
        raw, h = raws[parity], h_ref[...]
        raw[SUBLANES:, :tn] = _bf16_dot(h, wb_ref[...])
        raw[SUBLANES:, p_cols] = _bf16_dot(h, wc_ref[...]) * _bf16_dot(h, wx_ref[...])

    def gate(parity):
        raw, act = raws[parity], acts[parity]
        _swap_halo(raw, carry_ref.at[j - 1], p_cols, tm)
        zero_bits, tokens = zero_ref[...], [None] * GATE_CHAINS
        for c in range(tn // LANES):
            cs = slice(c * LANES, (c + 1) * LANES)
            w = cw_ref[:, cs]
            for r in range(tm // GATE_ROWS):
                r0, token = SUBLANES + r * GATE_ROWS, tokens[r % GATE_CHAINS]
                conv = _conv3_rows(raw, r0, GATE_ROWS, slice(tn + c * LANES, tn + (c + 1) * LANES), w, token)
                y = _after(raw[r0:r0 + GATE_ROWS, cs], token) * conv
                tokens[r % GATE_CHAINS] = _token(y, zero_bits)
                act[r * GATE_ROWS:(r + 1) * GATE_ROWS, cs] = y.astype(act.dtype)

    def contract(parity):
        o_ref[...] += _bf16_dot(acts[parity][...], wout_ref[...])

    _three_stage_step(j, nj, first, project, gate, contract, lambda: None)


def _short_conv_block(x, g, w_in, w_conv, w_out, *, seq, cast=()):
    rows, d = x.shape
    tm, tn = CONV_ROWS, CONV_TILE
    nj = d // tn
    grid = (rows // tm, nj + 2)
    cast_in, cast_out, cast_shapes, cast_counts = _cast_specs(cast, grid[0] * grid[1], grid[1])
    body = functools.partial(_short_conv_kernel, tiles_per_seq=seq // tm, tn=tn, nj=nj,
                             cast_counts=cast_counts)
    return pl.pallas_call(
        body,
        grid=grid,
        in_specs=[
            pl.BlockSpec((tm, d), _step0_rows(grid[0])),
            pl.BlockSpec((1, d), lambda i, j: (0, 0)),
            pl.BlockSpec((SUBLANES, LANES), lambda i, j: (0, 0)),
            pl.BlockSpec((d, tn), lambda i, j: (0, _stage_block(j, 0, nj))),
            pl.BlockSpec((d, tn), lambda i, j: (0, nj + _stage_block(j, 0, nj))),
            pl.BlockSpec((d, tn), lambda i, j: (0, 2 * nj + _stage_block(j, 0, nj))),
            pl.BlockSpec((3, tn), lambda i, j: (0, _stage_block(j, 1, nj))),
            pl.BlockSpec((tn, d), lambda i, j: (_stage_block(j, 2, nj), 0)),
            *cast_in,
        ],
        out_specs=[pl.BlockSpec((tm, d), lambda i, j: (i, 0)), *cast_out],
        out_shape=[jax.ShapeDtypeStruct((rows, d), jnp.float32), *cast_shapes],
        scratch_shapes=[
            pltpu.VMEM((tm, d), jnp.bfloat16),
            pltpu.VMEM((tm + SUBLANES, 2 * tn), jnp.float32),
            pltpu.VMEM((tm + SUBLANES, 2 * tn), jnp.float32),
            pltpu.VMEM((tm, tn), jnp.bfloat16),
            pltpu.VMEM((tm, tn), jnp.bfloat16),
            pltpu.VMEM((nj, SUBLANES, tn), jnp.float32),
        ],
        compiler_params=pltpu.CompilerParams(
            dimension_semantics=("arbitrary", "arbitrary"),
            vmem_limit_bytes=VMEM_LIMIT_BYTES),
        name="short_conv_mixer",
    )(x, g.reshape(1, d), _zero_bits(), w_in, w_in, w_in, w_conv, w_out, *(stack for stack, _ in cast))


def _conv_ffn_kernel(*refs, tiles_per_seq, tf, nj, final_norm, cast_counts):
    nc = len(cast_counts)
    ((x_ref, g_ref, zero_ref, wg_ref, wa_ref, cwg_ref, cwa_ref, cbg_ref, cba_ref, wdown_ref, fg_ref),
     cast_srcs, (o_ref,), cast_dsts, (h_ref, raw_even, raw_odd, act_even, act_odd, carry_ref)) = _take(
        refs, 11, nc, 1, nc, 6)
    _cast_panels(cast_srcs, cast_dsts, cast_counts)
    i, j = pl.program_id(0), pl.program_id(1)
    raws, acts = (raw_even, raw_odd), (act_even, act_odd)
    tm = h_ref.shape[0]

    def first():
        x = x_ref[...]
        h_ref[...] = _rmsnorm(x, g_ref[...]).astype(h_ref.dtype)
        o_ref[...] = x

        @pl.when(i % tiles_per_seq == 0)
        def _():
            carry_ref[...] = jnp.zeros(carry_ref.shape, carry_ref.dtype)

    def project(parity):
        raw, h = raws[parity], h_ref[...]
        raw[SUBLANES:, :tf] = _bf16_dot(h, wg_ref[...])
        raw[SUBLANES:, tf:] = _bf16_dot(h, wa_ref[...])

    def gate(parity):
        raw, act = raws[parity], acts[parity]
        _swap_halo(raw, carry_ref.at[j - 1], slice(None), tm)
        zero_bits, tokens = zero_ref[...], [None] * GATE_CHAINS
        for c in range(tf // LANES):
            cs = slice(c * LANES, (c + 1) * LANES)
            wts = [(cw_ref[:, cs], cb_ref[:, cs]) for cw_ref, cb_ref in ((cwg_ref, cbg_ref), (cwa_ref, cba_ref))]
            for r in range(tm // GATE_ROWS):
                r0, token = SUBLANES + r * GATE_ROWS, tokens[r % GATE_CHAINS]
                ga = [_conv3_rows(raw, r0, GATE_ROWS, slice(p * tf + c * LANES, p * tf + (c + 1) * LANES), w, token) + b
                      for p, (w, b) in enumerate(wts)]
                y = jax.nn.silu(ga[0]) * ga[1]
                tokens[r % GATE_CHAINS] = _token(y, zero_bits)
                act[r * GATE_ROWS:(r + 1) * GATE_ROWS, cs] = y.astype(act.dtype)

    def contract(parity):
        o_ref[...] += _bf16_dot(acts[parity][...], wdown_ref[...])

    def last():
        if final_norm:
            o_ref[...] = _rmsnorm(o_ref[...], fg_ref[...])

    _three_stage_step(j, nj, first, project, gate, contract, last)


def _conv_ffn_block(x, g, w_up, conv_w, conv_b, w_down, final_g, *, seq, final_norm, cast=()):
    rows, d = x.shape
    d_ff = w_down.shape[0]
    tm, tf = FFN_ROWS, FFN_TILE
    nj = d_ff // tf
    grid = (rows // tm, nj + 2)
    cb = conv_b.reshape(1, 2 * d_ff)
    cast_in, cast_out, cast_shapes, cast_counts = _cast_specs(cast, grid[0] * grid[1], grid[1])
    body = functools.partial(_conv_ffn_kernel, tiles_per_seq=seq // tm, tf=tf, nj=nj,
                             final_norm=final_norm, cast_counts=cast_counts)
    return pl.pallas_call(
        body,
        grid=grid,
        in_specs=[
            pl.BlockSpec((tm, d), _step0_rows(grid[0])),
            pl.BlockSpec((1, d), lambda i, j: (0, 0)),
            pl.BlockSpec((SUBLANES, LANES), lambda i, j: (0, 0)),
            pl.BlockSpec((d, tf), lambda i, j: (0, _stage_block(j, 0, nj))),
            pl.BlockSpec((d, tf), lambda i, j: (0, nj + _stage_block(j, 0, nj))),
            pl.BlockSpec((3, tf), lambda i, j: (0, _stage_block(j, 1, nj))),
            pl.BlockSpec((3, tf), lambda i, j: (0, nj + _stage_block(j, 1, nj))),
            pl.BlockSpec((1, tf), lambda i, j: (0, _stage_block(j, 1, nj))),
            pl.BlockSpec((1, tf), lambda i, j: (0, nj + _stage_block(j, 1, nj))),
            pl.BlockSpec((tf, d), lambda i, j: (_stage_block(j, 2, nj), 0)),
            pl.BlockSpec((1, d), lambda i, j: (0, 0)),
            *cast_in,
        ],
        out_specs=[pl.BlockSpec((tm, d), lambda i, j: (i, 0)), *cast_out],
        out_shape=[jax.ShapeDtypeStruct((rows, d), jnp.float32), *cast_shapes],
        scratch_shapes=[
            pltpu.VMEM((tm, d), jnp.bfloat16),
            pltpu.VMEM((tm + SUBLANES, 2 * tf), jnp.float32),
            pltpu.VMEM((tm + SUBLANES, 2 * tf), jnp.float32),
            pltpu.VMEM((tm, tf), jnp.bfloat16),
            pltpu.VMEM((tm, tf), jnp.bfloat16),
            pltpu.VMEM((nj, SUBLANES, 2 * tf), jnp.float32),
        ],
        compiler_params=pltpu.CompilerParams(
            dimension_semantics=("arbitrary", "arbitrary"),
            vmem_limit_bytes=VMEM_LIMIT_BYTES),
        name="conv_ffn",
    )(x, g.reshape(1, d), _zero_bits(), w_up, w_up, conv_w, conv_w, cb, cb,
      w_down, final_g.reshape(1, d), *(stack for stack, _ in cast))


def _spatial_gate_kernel(x_ref, g_ref, win_ref, vg_ref, ws_ref, bst_ref, wout_ref, o_ref,
                         h_ref, uv_ref, vn_ref, ug_ref, *, groups):
    n_in, tm, tc = uv_ref.shape
    width = wout_ref.shape[0]
    half_blocks = n_in // 2
    gw = width // groups
    groups_per_block = tc // gw

    h_ref[...] = _rmsnorm(x_ref[...], g_ref[...]).astype(h_ref.dtype)

    def project(c):
        uv_ref[c] = _bf16_dot(h_ref[...], win_ref[:, c * tc:(c + 1) * tc])

    def gelu(c):
        uv_ref[c] = jax.nn.gelu(uv_ref[c])

    def normalise_v():
        ssq = jnp.zeros((tm, 1), jnp.float32)
        for b in range(half_blocks):
            v = uv_ref[half_blocks + b]
            ssq += jnp.sum(v * v, axis=-1, keepdims=True)
        inv = lax.rsqrt(ssq / width + RMS_EPS)
        for b in range(half_blocks):
            cs = slice(b * tc, (b + 1) * tc)
            vn_ref[:, cs] = ((uv_ref[half_blocks + b] * inv) * vg_ref[:, cs]).astype(vn_ref.dtype)

    order = list(range(half_blocks, n_in)) + list(range(half_blocks))
    project(order[0])
    for prev, c in zip(order, order[1:]):
        gelu(prev)
        if prev == n_in - 1:
            normalise_v()
        project(c)
    gelu(order[-1])

    row = lax.broadcasted_iota(jnp.int32, (CHUNK, CHUNK), 0)
    col = lax.broadcasted_iota(jnp.int32, (CHUNK, CHUNK), 1)
    tril = (row >= col).astype(jnp.float32)
    for b in range(half_blocks):
        for gl in range(groups_per_block):
            hg = b * groups_per_block + gl
            ws = (ws_ref[hg] * tril).astype(jnp.bfloat16)
            bias = bst_ref[:, hg:hg + 1]
            for c in range(tm // CHUNK):
                rs = slice(c * CHUNK, (c + 1) * CHUNK)
                gate = _bf16_dot(ws, vn_ref[rs, hg * gw:(hg + 1) * gw]) + bias
                u = uv_ref[b, rs, gl * gw:(gl + 1) * gw]
                ug_ref[rs, hg * gw:(hg + 1) * gw] = (u * gate).astype(ug_ref.dtype)

    for k in range(o_ref.shape[1] // tc):
        cs = slice(k * tc, (k + 1) * tc)
        o_ref[:, cs] = x_ref[:, cs] + _bf16_dot(ug_ref[...], wout_ref[:, cs])


def _spatial_gate_block(x, g, w_in, v_norm, w_s, b_s, w_out):
    rows, d = x.shape
    width = w_out.shape[0]
    groups = w_s.shape[0]
    tm, tc = SG_ROWS, SG_TILE
    resident = dict(pipeline_mode=pl.Buffered(1))
    body = functools.partial(_spatial_gate_kernel, groups=groups)
    return pl.pallas_call(
        body,
        grid=(rows // tm,),
        in_specs=[
            pl.BlockSpec((tm, d), lambda i: (i, 0)),
            pl.BlockSpec((1, d), lambda i: (0, 0)),
            pl.BlockSpec((d, 2 * width), lambda i: (0, 0), **resident),
            pl.BlockSpec((1, width), lambda i: (0, 0)),
            pl.BlockSpec((groups, CHUNK, CHUNK), lambda i: (0, 0, 0)),
            pl.BlockSpec((CHUNK, groups), lambda i: (0, 0)),
            pl.BlockSpec((width, d), lambda i: (0, 0), **resident),
        ],
        out_specs=pl.BlockSpec((tm, d), lambda i: (i, 0)),
        out_shape=jax.ShapeDtypeStruct((rows, d), jnp.float32),
        scratch_shapes=[
            pltpu.VMEM((tm, d), jnp.bfloat16),
            pltpu.VMEM((2 * width // tc, tm, tc), jnp.float32),
            pltpu.VMEM((tm, width), jnp.bfloat16),
            pltpu.VMEM((tm, width), jnp.bfloat16),
        ],
        compiler_params=pltpu.CompilerParams(
            dimension_semantics=("arbitrary",),
            vmem_limit_bytes=VMEM_LIMIT_BYTES),
        name="spatial_gate_mixer",
    )(x, g.reshape(1, d), w_in, v_norm.reshape(1, width), w_s, b_s.T, w_out)


def kernel(x, a_norm, a_in, a_conv, a_out, b_norm, b_in, b_vnorm, b_ws, b_bs, b_out,
           f_norm, f_up, f_conv_w, f_conv_b, f_down, final_norm):
    batch, seq, d = x.shape
    depth = f_norm.shape[0]
    assert seq % FFN_ROWS == 0 and seq % CONV_ROWS == 0 and seq % SG_ROWS == 0 and SG_ROWS % CHUNK == 0

    def mixer_weights(layer):
        m = layer // 2
        return ((a_in, m), (a_out, m)) if layer % 2 == 0 else ((b_in, m), (b_out, m))

    h = x.reshape(batch * seq, d)
    w_mix = tuple(stack[k].astype(jnp.bfloat16) for stack, k in mixer_weights(0))
    w_ffn = ()
    for layer in range(depth):
        m, nxt = layer // 2, layer + 1
        to_cast = () if w_ffn else ((f_up, layer), (f_down, layer))
        if layer % 2 == 0:
            h, *done = _short_conv_block(h, a_norm[m], w_mix[0], a_conv[m], w_mix[1], seq=seq, cast=to_cast)
            w_ffn = w_ffn or tuple(done)
        else:
            assert not to_cast
            h = _spatial_gate_block(h, b_norm[m], w_mix[0], b_vnorm[m], b_ws[m], b_bs[m], w_mix[1])
        to_cast = (*mixer_weights(nxt), (f_up, nxt), (f_down, nxt)) if nxt < depth else ()
        h, *done = _conv_ffn_block(h, f_norm[layer], w_ffn[0], f_conv_w[layer], f_conv_b[layer], w_ffn[1],
                                   final_norm, seq=seq, final_norm=(nxt == depth), cast=to_cast)
        w_mix, w_ffn = tuple(done[:2]), tuple(done[2:])
    return h.reshape(batch, seq, d)
```

```python
import functools

import jax
import jax.numpy as jnp
from jax import lax
from jax.experimental import pallas as pl
from jax.experimental.pallas import tpu as pltpu

RMS_EPS = 1e-5
CHUNK = 128
SUBLANES = 8
LANES = 128
FFN_ROWS = 1024
FFN_TILE = 256
CONV_ROWS = 1024
CONV_TILE = 256
SG_ROWS = 512
SG_TILE = 2048
GATE_ROWS = 16
X_NEXT_STEP = 5
GATE_CHAINS = 2
VMEM_LIMIT_BYTES = 58 * 1024 * 1024


def _rmsnorm(x, g):
    inv = lax.rsqrt(jnp.mean(x * x, axis=-1, keepdims=True) + RMS_EPS)
    return (x * inv) * g


def _bf16_dot(a, b):
    return jnp.dot(a, b, preferred_element_type=jnp.float32)


def _panel_rows(rows, n_steps):
    return next(pr for pr in range(2 * SUBLANES, rows + 1, 2 * SUBLANES)
                if rows % pr == 0 and rows // pr <= n_steps)


def _cast_specs(weights, n_steps, steps_per_tile):
    in_specs, out_specs, shapes, counts = [], [], [], []
    for stack, layer in weights:
        _, rows, cols = stack.shape
        pr = _panel_rows(rows, n_steps)
        n = rows // pr

        def panel(i, j, n=n):
            return jnp.minimum(i * steps_per_tile + j, n - 1)

        in_specs.append(pl.BlockSpec((None, pr, cols), lambda i, j, layer=layer, panel=panel: (layer, panel(i, j), 0)))
        out_specs.append(pl.BlockSpec((pr, cols), lambda i, j, panel=panel: (panel(i, j), 0)))
        shapes.append(jax.ShapeDtypeStruct((rows, cols), jnp.bfloat16))
        counts.append(n)
    return in_specs, out_specs, shapes, tuple(counts)


def _cast_panels(srcs, dsts, counts):
    step = pl.program_id(0) * pl.num_programs(1) + pl.program_id(1)
    for src, dst, n in zip(srcs, dsts, counts):
        @pl.when(step < n)
        def _(src=src, dst=dst):
            dst[...] = src[...].astype(dst.dtype)


def _take(refs, *counts):
    it = iter(refs)
    return [tuple(next(it) for _ in range(c)) for c in counts]


def _three_stage_step(j, nj, first, project, gate, contract, last):
    def step(parity, do_project, do_gate, do_contract):
        if do_gate:
            gate(1 - parity)
        if do_project:
            project(parity)
        if do_contract:
            contract(parity)

    @pl.when(j == 0)
    def _():
        first()
        step(0, True, False, False)

    @pl.when(j == 1)
    def _():
        step(1, True, True, False)

    for parity in range(2):
        @pl.when((j >= 2) & (j < nj) & (j % 2 == parity))
        def _(parity=parity):
            step(parity, True, True, True)

    @pl.when(j == nj)
    def _():
        step(nj % 2, False, True, True)

    @pl.when(j == nj + 1)
    def _():
        step((nj + 1) % 2, False, False, True)
        last()


def _step0_rows(n_tiles):
    return lambda i, j: (jnp.minimum(i + (j >= X_NEXT_STEP), n_tiles - 1), 0)


def _stage_block(j, lag, nj):
    return jnp.clip(j - lag, 0, nj - 1)


def _swap_halo(raw, tail, cols, tm):
    raw[:SUBLANES, cols] = tail[...]
    tail[...] = raw[tm:, cols]


def _conv3_rows(raw, r0, rb, cols, w, token):
    win = _after(raw[r0 - SUBLANES:r0 + rb, cols], token)
    lo = SUBLANES
    return win[lo - 2:lo - 2 + rb] * w[0:1] + win[lo - 1:lo - 1 + rb] * w[1:2] + win[lo:] * w[2:3]


def _zero_bits():
    return jnp.zeros((SUBLANES, LANES), jnp.int32)


def _token(y, zero_bits):
    bits = lax.bitcast_convert_type(y[:SUBLANES], jnp.int32) & zero_bits
    return lax.bitcast_convert_type(bits, jnp.float32)


def _after(x, token):
    if token is None:
        return x
    return x + jnp.concatenate([token] * (x.shape[0] // SUBLANES), axis=0)


def _short_conv_kernel(*refs, tiles_per_seq, tn, nj, cast_counts):
    nc = len(cast_counts)
    ((x_ref, g_ref, zero_ref, wb_ref, wc_ref, wx_ref, cw_ref, wout_ref), cast_srcs, (o_ref,), cast_dsts,
     (h_ref, raw_even, raw_odd, act_even, act_odd, carry_ref)) = _take(refs, 8, nc, 1, nc, 6)
    _cast_panels(cast_srcs, cast_dsts, cast_counts)
    i, j = pl.program_id(0), pl.program_id(1)
    raws, acts = (raw_even, raw_odd), (act_even, act_odd)
    tm = h_ref.shape[0]
    p_cols = slice(tn, 2 * tn)

    def first():
        x = x_ref[...]
        h_ref[...] = _rmsnorm(x, g_ref[...]).astype(h_ref.dtype)
        o_ref[...] = x

        @pl.when(i % tiles_per_seq == 0)
        def _():
            carry_ref[...] = jnp.zeros(carry_ref.shape, carry_ref.dtype)

    def project(parity):
        raw, h = raws[parity], h_ref[...]
        raw[SUBLANES:, :tn] = _bf16_dot(h, wb_ref[...])
        raw[SUBLANES:, p_cols] = _bf16_dot(h, wc_ref[...]) * _bf16_dot(h, wx_ref[...])

    def gate(parity):
        raw, act = raws[parity], acts[parity]
        _swap_halo(raw, carry_ref.at[j - 1], p_cols, tm)
        zero_bits, tokens = zero_ref[...], [None] * GATE_CHAINS
        for c in range(tn // LANES):
            cs = slice(c * LANES, (c + 1) * LANES)
            w = cw_ref[:, cs]
            for r in range(tm // GATE_ROWS):
                r0, token = SUBLANES + r * GATE_ROWS, tokens[r % GATE_CHAINS]
                conv = _conv3_rows(raw, r0, GATE_ROWS, slice(tn + c * LANES, tn + (c + 1) * LANES), w, token)
                y = _after(raw[r0:r0 + GATE_ROWS, cs], token) * conv
                tokens[r % GATE_CHAINS] = _token(y, zero_bits)
                act[r * GATE_ROWS:(r + 1) * GATE_ROWS, cs] = y.astype(act.dtype)

    def contract(parity):
        o_ref[...] += _bf16_dot(acts[parity][...], wout_ref[...])

    _three_stage_step(j, nj, first, project, gate, contract, lambda: None)


def _short_conv_block(x, g, w_in, w_conv, w_out, *, seq, cast=()):
    rows, d = x.shape
    tm, tn = CONV_ROWS, CONV_TILE
    nj = d // tn
    grid = (rows // tm, nj + 2)
    cast_in, cast_out, cast_shapes, cast_counts = _cast_specs(cast, grid[0] * grid[1], grid[1])
    body = functools.partial(_short_conv_kernel, tiles_per_seq=seq // tm, tn=tn, nj=nj,
                             cast_counts=cast_counts)
    return pl.pallas_call(
        body,
        grid=grid,
        in_specs=[
            pl.BlockSpec((tm, d), _step0_rows(grid[0])),
            pl.BlockSpec((1, d), lambda i, j: (0, 0)),
            pl.BlockSpec((SUBLANES, LANES), lambda i, j: (0, 0)),
            pl.BlockSpec((d, tn), lambda i, j: (0, _stage_block(j, 0, nj))),
            pl.BlockSpec((d, tn), lambda i, j: (0, nj + _stage_block(j, 0, nj))),
            pl.BlockSpec((d, tn), lambda i, j: (0, 2 * nj + _stage_block(j, 0, nj))),
            pl.BlockSpec((3, tn), lambda i, j: (0, _stage_block(j, 1, nj))),
            pl.BlockSpec((tn, d), lambda i, j: (_stage_block(j, 2, nj), 0)),
            *cast_in,
        ],
        out_specs=[pl.BlockSpec((tm, d), lambda i, j: (i, 0)), *cast_out],
        out_shape=[jax.ShapeDtypeStruct((rows, d), jnp.float32), *cast_shapes],
        scratch_shapes=[
            pltpu.VMEM((tm, d), jnp.bfloat16),
            pltpu.VMEM((tm + SUBLANES, 2 * tn), jnp.float32),
            pltpu.VMEM((tm + SUBLANES, 2 * tn), jnp.float32),
            pltpu.VMEM((tm, tn), jnp.bfloat16),
            pltpu.VMEM((tm, tn), jnp.bfloat16),
            pltpu.VMEM((nj, SUBLANES, tn), jnp.float32),
        ],
        compiler_params=pltpu.CompilerParams(
            dimension_semantics=("arbitrary", "arbitrary"),
            vmem_limit_bytes=VMEM_LIMIT_BYTES),
        name="short_conv_mixer",
    )(x, g.reshape(1, d), _zero_bits(), w_in, w_in, w_in, w_conv, w_out, *(stack for stack, _ in cast))


def _conv_ffn_kernel(*refs, tiles_per_seq, tf, nj, final_norm, cast_counts):
    nc = len(cast_counts)
    ((x_ref, g_ref, zero_ref, wg_ref, wa_ref, cg_ref, ca_ref, wdown_ref, fg_ref),
     cast_srcs, (o_ref,), cast_dsts, (h_ref, raw_even, raw_odd, act_even, act_odd, carry_ref)) = _take(
        refs, 9, nc, 1, nc, 6)
    _cast_panels(cast_srcs, cast_dsts, cast_counts)
    i, j = pl.program_id(0), pl.program_id(1)
    raws, acts = (raw_even, raw_odd), (act_even, act_odd)
    tm = h_ref.shape[0]

    def first():
        x = x_ref[...]
        h_ref[...] = _rmsnorm(x, g_ref[...]).astype(h_ref.dtype)
        o_ref[...] = x

        @pl.when(i % tiles_per_seq == 0)
        def _():
            carry_ref[...] = jnp.zeros(carry_ref.shape, carry_ref.dtype)

    def project(parity):
        raw, h = raws[parity], h_ref[...]
        raw[SUBLANES:, :tf] = _bf16_dot(h, wg_ref[...])
        raw[SUBLANES:, tf:] = _bf16_dot(h, wa_ref[...])

    def gate(parity):
        raw, act = raws[parity], acts[parity]
        _swap_halo(raw, carry_ref.at[j - 1], slice(None), tm)
        zero_bits, tokens = zero_ref[...], [None] * GATE_CHAINS
        for c in range(tf // LANES):
            cs = slice(c * LANES, (c + 1) * LANES)
            wts = [(c_ref[0:3, cs], c_ref[3:4, cs]) for c_ref in (cg_ref, ca_ref)]
            for r in range(tm // GATE_ROWS):
                r0, token = SUBLANES + r * GATE_ROWS, tokens[r % GATE_CHAINS]
                ga = [_conv3_rows(raw, r0, GATE_ROWS, slice(p * tf + c * LANES, p * tf + (c + 1) * LANES), w, token) + b
                      for p, (w, b) in enumerate(wts)]
                y = jax.nn.silu(ga[0]) * ga[1]
                tokens[r % GATE_CHAINS] = _token(y, zero_bits)
                act[r * GATE_ROWS:(r + 1) * GATE_ROWS, cs] = y.astype(act.dtype)

    def contract(parity):
        o_ref[...] += _bf16_dot(acts[parity][...], wdown_ref[...])

    def last():
        if final_norm:
            o_ref[...] = _rmsnorm(o_ref[...], fg_ref[...])

    _three_stage_step(j, nj, first, project, gate, contract, last)


def _conv_ffn_block(x, g, w_up, conv_w, conv_b, w_down, final_g, *, seq, final_norm, cast=()):
    rows, d = x.shape
    d_ff = w_down.shape[0]
    tm, tf = FFN_ROWS, FFN_TILE
    nj = d_ff // tf
    grid = (rows // tm, nj + 2)
    conv = jnp.concatenate([conv_w, conv_b.reshape(1, 2 * d_ff)], axis=0)
    cast_in, cast_out, cast_shapes, cast_counts = _cast_specs(cast, grid[0] * grid[1], grid[1])
    body = functools.partial(_conv_ffn_kernel, tiles_per_seq=seq // tm, tf=tf, nj=nj,
                             final_norm=final_norm, cast_counts=cast_counts)
    return pl.pallas_call(
        body,
        grid=grid,
        in_specs=[
            pl.BlockSpec((tm, d), _step0_rows(grid[0])),
            pl.BlockSpec((1, d), lambda i, j: (0, 0)),
            pl.BlockSpec((SUBLANES, LANES), lambda i, j: (0, 0)),
            pl.BlockSpec((d, tf), lambda i, j: (0, _stage_block(j, 0, nj))),
            pl.BlockSpec((d, tf), lambda i, j: (0, nj + _stage_block(j, 0, nj))),
            pl.BlockSpec((4, tf), lambda i, j: (0, _stage_block(j, 1, nj))),
            pl.BlockSpec((4, tf), lambda i, j: (0, nj + _stage_block(j, 1, nj))),
            pl.BlockSpec((tf, d), lambda i, j: (_stage_block(j, 2, nj), 0)),
            pl.BlockSpec((1, d), lambda i, j: (0, 0)),
            *cast_in,
        ],
        out_specs=[pl.BlockSpec((tm, d), lambda i, j: (i, 0)), *cast_out],
        out_shape=[jax.ShapeDtypeStruct((rows, d), jnp.float32), *cast_shapes],
        scratch_shapes=[
            pltpu.VMEM((tm, d), jnp.bfloat16),
            pltpu.VMEM((tm + SUBLANES, 2 * tf), jnp.float32),
            pltpu.VMEM((tm + SUBLANES, 2 * tf), jnp.float32),
            pltpu.VMEM((tm, tf), jnp.bfloat16),
            pltpu.VMEM((tm, tf), jnp.bfloat16),
            pltpu.VMEM((nj, SUBLANES, 2 * tf), jnp.float32),
        ],
        compiler_params=pltpu.CompilerParams(
            dimension_semantics=("arbitrary", "arbitrary"),
            vmem_limit_bytes=VMEM_LIMIT_BYTES),
        name="conv_ffn",
    )(x, g.reshape(1, d), _zero_bits(), w_up, w_up, conv, conv,
      w_down, final_g.reshape(1, d), *(stack for stack, _ in cast))


def _spatial_gate_kernel(x_ref, g_ref, win_ref, vg_ref, ws_ref, bst_ref, wout_ref, o_ref,
                         h_ref, uv_ref, ug_ref, *, groups):
    n_in, tm, tc = uv_ref.shape
    width = wout_ref.shape[0]
    half_blocks = n_in // 2
    gw = width // groups
    groups_per_block = tc // gw

    h_ref[...] = _rmsnorm(x_ref[...], g_ref[...]).astype(h_ref.dtype)

    def project(c):
        uv_ref[c] = _bf16_dot(h_ref[...], win_ref[:, c * tc:(c + 1) * tc])

    def gelu(c):
        uv_ref[c] = jax.nn.gelu(uv_ref[c])

    project(0)
    for c in range(1, n_in):
        gelu(c - 1)
        project(c)
    gelu(n_in - 1)

    ssq = jnp.zeros((tm, 1), jnp.float32)
    for b in range(half_blocks):
        v = uv_ref[half_blocks + b]
        ssq += jnp.sum(v * v, axis=-1, keepdims=True)
    inv = lax.rsqrt(ssq / width + RMS_EPS)
    row = lax.broadcasted_iota(jnp.int32, (CHUNK, CHUNK), 0)
    col = lax.broadcasted_iota(jnp.int32, (CHUNK, CHUNK), 1)
    tril = (row >= col).astype(jnp.float32)
    for b in range(half_blocks):
        vn = ((uv_ref[half_blocks + b] * inv) * vg_ref[:, b * tc:(b + 1) * tc]).astype(jnp.bfloat16)
        for gl in range(groups_per_block):
            hg = b * groups_per_block + gl
            ws = (ws_ref[hg] * tril).astype(jnp.bfloat16)
            bias = bst_ref[:, hg:hg + 1]
            for c in range(tm // CHUNK):
                rs = slice(c * CHUNK, (c + 1) * CHUNK)
                gate = _bf16_dot(ws, vn[rs, gl * gw:(gl + 1) * gw]) + bias
                u = uv_ref[b, rs, gl * gw:(gl + 1) * gw]
                ug_ref[rs, hg * gw:(hg + 1) * gw] = (u * gate).astype(ug_ref.dtype)

    for k in range(o_ref.shape[1] // tc):
        cs = slice(k * tc, (k + 1) * tc)
        o_ref[:, cs] = x_ref[:, cs] + _bf16_dot(ug_ref[...], wout_ref[:, cs])


def _spatial_gate_block(x, g, w_in, v_norm, w_s, b_s, w_out):
    rows, d = x.shape
    width = w_out.shape[0]
    groups = w_s.shape[0]
    tm, tc = SG_ROWS, SG_TILE
    resident = dict(pipeline_mode=pl.Buffered(1))
    body = functools.partial(_spatial_gate_kernel, groups=groups)
    return pl.pallas_call(
        body,
        grid=(rows // tm,),
        in_specs=[
            pl.BlockSpec((tm, d), lambda i: (i, 0)),
            pl.BlockSpec((1, d), lambda i: (0, 0)),
            pl.BlockSpec((d, 2 * width), lambda i: (0, 0), **resident),
            pl.BlockSpec((1, width), lambda i: (0, 0)),
            pl.BlockSpec((groups, CHUNK, CHUNK), lambda i: (0, 0, 0)),
            pl.BlockSpec((CHUNK, groups), lambda i: (0, 0)),
            pl.BlockSpec((width, d), lambda i: (0, 0), **resident),
        ],
        out_specs=pl.BlockSpec((tm, d), lambda i: (i, 0)),
        out_shape=jax.ShapeDtypeStruct((rows, d), jnp.float32),
        scratch_shapes=[
            pltpu.VMEM((tm, d), jnp.bfloat16),
            pltpu.VMEM((2 * width // tc, tm, tc), jnp.float32),
            pltpu.VMEM((tm, width), jnp.bfloat16),
        ],
        compiler_params=pltpu.CompilerParams(
            dimension_semantics=("arbitrary",),
            vmem_limit_bytes=VMEM_LIMIT_BYTES),
        name="spatial_gate_mixer",
    )(x, g.reshape(1, d), w_in, v_norm.reshape(1, width), w_s, b_s.T, w_out)


def kernel(x, a_norm, a_in, a_conv, a_out, b_norm, b_in, b_vnorm, b_ws, b_bs, b_out,
           f_norm, f_up, f_conv_w, f_conv_b, f_down, final_norm):
    batch, seq, d = x.shape
    depth = f_norm.shape[0]
    assert seq % FFN_ROWS == 0 and seq % CONV_ROWS == 0 and seq % SG_ROWS == 0 and SG_ROWS % CHUNK == 0

    def mixer_weights(layer):
        m = layer // 2
        return ((a_in, m), (a_out, m)) if layer % 2 == 0 else ((b_in, m), (b_out, m))

    h = x.reshape(batch * seq, d)
    w_mix = tuple(stack[k].astype(jnp.bfloat16) for stack, k in mixer_weights(0))
    w_ffn = ()
    for layer in range(depth):
        m, nxt = layer // 2, layer + 1
        to_cast = () if w_ffn else ((f_up, layer), (f_down, layer))
        if layer % 2 == 0:
            h, *done = _short_conv_block(h, a_norm[m], w_mix[0], a_conv[m], w_mix[1], seq=seq, cast=to_cast)
            w_ffn = w_ffn or tuple(done)
        else:
            assert not to_cast
            h = _spatial_gate_block(h, b_norm[m], w_mix[0], b_vnorm[m], b_ws[m], b_bs[m], w_mix[1])
        to_cast = (*mixer_weights(nxt), (f_up, nxt), (f_down, nxt)) if nxt < depth else ()
        h, *done = _conv_ffn_block(h, f_norm[layer], w_ffn[0], f_conv_w[layer], f_conv_b[layer], w_ffn[1],
                                   final_norm, seq=seq, final_norm=(nxt == depth), cast=to_cast)
        w_mix, w_ffn = tuple(done[:2]), tuple(done[2:])
    return h.reshape(batch, seq, d)
```
